```python
import jax, jax.numpy as jnp
from jax import lax
import numpy as np

D_MODEL = 1024
BATCH = 8
SEQ = 2048
DEPTH = 4

ATTN_HEADS = 8
HEAD_DIM = 64
ATTN_WIDTH = ATTN_HEADS * HEAD_DIM
CONV_WIDTH = D_MODEL - ATTN_WIDTH
CONV_KERNEL = 31
D_FF = 2816
FFN_KERNEL = 3
BLOCK_Q = 128
IN_WIDTH = 3 * ATTN_WIDTH + 2 * CONV_WIDTH
EPS = 1e-6

kernel_name = "hybrid_stickbreak_conformer_convffn"


def rms_norm(x, g):
    xf = x.astype(jnp.float32)
    y = xf * lax.rsqrt(jnp.mean(xf * xf, axis=-1, keepdims=True) + EPS)
    return (y * g.astype(jnp.float32)).astype(x.dtype)


def layer_norm(x, g, b):
    xf = x.astype(jnp.float32)
    mu = jnp.mean(xf, axis=-1, keepdims=True)
    xc = xf - mu
    var = jnp.mean(xc * xc, axis=-1, keepdims=True)
    y = xc * lax.rsqrt(var + EPS) * g.astype(jnp.float32) + b.astype(jnp.float32)
    return y.astype(x.dtype)


def causal_depthwise_conv(x, w, b):
    k_width, channels = w.shape
    y = lax.conv_general_dilated(
        x, w[:, None, :].astype(x.dtype),
        window_strides=(1,), padding=[(k_width - 1, 0)],
        dimension_numbers=("NWC", "WIO", "NWC"),
        feature_group_count=channels)
    return y + b.astype(x.dtype)


def stick_breaking_attention(q, k, v):
    seq = q.shape[1]
    scale = q.shape[-1] ** -0.5
    outs = []
    for start in range(0, seq, BLOCK_Q):
        end = min(start + BLOCK_Q, seq)
        qb = q[:, start:end].astype(jnp.float32)
        kb = k[:, :end].astype(jnp.float32)
        vb = v[:, :end].astype(jnp.float32)
        z = jnp.einsum("bqhd,bkhd->bhqk", qb, kb) * scale
        t_idx = start + jnp.arange(end - start)[:, None]
        s_idx = jnp.arange(end)[None, :]
        mask = s_idx < t_idx
        log_beta = jax.nn.log_sigmoid(z)
        log_one_minus = jnp.where(mask, jax.nn.log_sigmoid(-z), 0.0)
        tail = lax.cumsum(log_one_minus, axis=3, reverse=True) - log_one_minus
        weights = jnp.where(mask, jnp.exp(log_beta + tail), 0.0)
        outs.append(jnp.einsum("bhqk,bkhd->bqhd", weights, vb))
    return jnp.concatenate(outs, axis=1).astype(v.dtype)


def setup_inputs(seed: int = 0) -> dict:
    key = jax.random.key(seed)
    ks = jax.random.split(key, 16)
    f32 = jnp.float32
    out_scale = (2.0 * DEPTH) ** -0.5

    def nrm(k, shape, s):
        return jax.random.normal(k, shape, f32) * s

    return {
        "x": jax.random.normal(ks[0], (BATCH, SEQ, D_MODEL), f32),
        "norm1_g": 1.0 + nrm(ks[1], (DEPTH, D_MODEL), 0.02),
        "w_in": nrm(ks[2], (DEPTH, D_MODEL, IN_WIDTH), D_MODEL ** -0.5),
        "q_norm_g": 1.0 + nrm(ks[3], (DEPTH, HEAD_DIM), 0.02),
        "k_norm_g": 1.0 + nrm(ks[4], (DEPTH, HEAD_DIM), 0.02),
        "conv_dw_w": nrm(ks[5], (DEPTH, CONV_KERNEL, CONV_WIDTH), CONV_KERNEL ** -0.5),
        "conv_dw_b": nrm(ks[6], (DEPTH, CONV_WIDTH), 0.01),
        "conv_ln_g": 1.0 + nrm(ks[7], (DEPTH, CONV_WIDTH), 0.02),
        "conv_ln_b": nrm(ks[8], (DEPTH, CONV_WIDTH), 0.01),
        "w_out": nrm(ks[9], (DEPTH, D_MODEL, D_MODEL), D_MODEL ** -0.5 * out_scale),
        "norm2_g": 1.0 + nrm(ks[10], (DEPTH, D_MODEL), 0.02),
        "w_up": nrm(ks[11], (DEPTH, D_MODEL, 2 * D_FF), D_MODEL ** -0.5),
        "ffn_dw_w": nrm(ks[12], (DEPTH, FFN_KERNEL, D_FF), FFN_KERNEL ** -0.5),
        "ffn_dw_b": nrm(ks[13], (DEPTH, D_FF), 0.01),
        "w_down": nrm(ks[14], (DEPTH, D_FF, D_MODEL), D_FF ** -0.5 * out_scale),
    }


def reference(x, norm1_g, w_in, q_norm_g, k_norm_g, conv_dw_w, conv_dw_b,
              conv_ln_g, conv_ln_b, w_out, norm2_g, w_up, ffn_dw_w, ffn_dw_b,
              w_down):
    bsz, seq, _ = x.shape
    splits = [ATTN_WIDTH, 2 * ATTN_WIDTH, 3 * ATTN_WIDTH, 3 * ATTN_WIDTH + CONV_WIDTH]
    for layer in range(DEPTH):
        h = rms_norm(x, norm1_g[layer])
        proj = h @ w_in[layer]
        q, k, v, glu_a, glu_b = jnp.split(proj, splits, axis=-1)
        q = rms_norm(q.reshape(bsz, seq, ATTN_HEADS, HEAD_DIM), q_norm_g[layer])
        k = rms_norm(k.reshape(bsz, seq, ATTN_HEADS, HEAD_DIM), k_norm_g[layer])
        v = v.reshape(bsz, seq, ATTN_HEADS, HEAD_DIM)
        attn = stick_breaking_attention(q, k, v).reshape(bsz, seq, ATTN_WIDTH)

        c = glu_a * jax.nn.sigmoid(glu_b)
        c = causal_depthwise_conv(c, conv_dw_w[layer], conv_dw_b[layer])
        c = jax.nn.silu(layer_norm(c, conv_ln_g[layer], conv_ln_b[layer]))

        mixed = jnp.concatenate([attn, c], axis=-1) @ w_out[layer]
        x = x + mixed

        h = rms_norm(x, norm2_g[layer])
        gate, val = jnp.split(h @ w_up[layer], 2, axis=-1)
        gate = jax.nn.silu(causal_depthwise_conv(gate, ffn_dw_w[layer], ffn_dw_b[layer]))
        x = x + (gate * val) @ w_down[layer]
    return x
```

```python
import functools

import jax
import jax.numpy as jnp
from jax import lax
from jax.experimental import pallas as pl
from jax.experimental.pallas import tpu as pltpu

F32 = jnp.float32
BF16 = jnp.bfloat16

EPS = 1e-6
HEAD_DIM = 64
LANES = 128
ATTN_TILE = 256
ROW_TILE = 512
CONV_HALO = 32
FFN_HALO = 16
FFN_CHUNK = 256
CONV_ROWS = 32
VMEM_LIMIT = 56 * 1024 * 1024


def _params(*sem):
    return pltpu.CompilerParams(dimension_semantics=sem, vmem_limit_bytes=VMEM_LIMIT)


def _dot(a, b):
    return jnp.dot(a, b, preferred_element_type=F32)


def _proj_body(x_ref, g1_ref, w_ref, qg_ref, kg_ref, bd_ref, q_ref, kt_ref, v_ref, c_ref, *, aw):
    x = x_ref[0]
    ms = jnp.mean(x * x, axis=-1, keepdims=True)
    h = (x * lax.rsqrt(ms + EPS) * g1_ref[0]).astype(BF16)

    def head_norm(t, g):
        sq = (t * t).astype(BF16)
        half = 2 * LANES
        ss = jnp.concatenate(
            [_dot(sq[:, i * half:(i + 1) * half], bd_ref[...]) for i in range(aw // half)], axis=-1)
        return t * lax.rsqrt(ss * (1.0 / HEAD_DIM) + EPS) * g

    q = head_norm(_dot(h, w_ref[0, :, 0:aw]), qg_ref[0])
    q_ref[0] = (q * (HEAD_DIM ** -0.5)).astype(BF16)
    k = head_norm(_dot(h, w_ref[0, :, aw:2 * aw]), kg_ref[0])
    for i in range(k.shape[0] // ATTN_TILE):
        kt_ref[0, i] = k[i * ATTN_TILE:(i + 1) * ATTN_TILE, :].T.astype(BF16)
    v_ref[0] = _dot(h, w_ref[0, :, 2 * aw:3 * aw]).astype(BF16)
    cw = (w_ref.shape[2] - 3 * aw) // 2
    a = _dot(h, w_ref[0, :, 3 * aw:3 * aw + cw])
    b = _dot(h, w_ref[0, :, 3 * aw + cw:3 * aw + 2 * cw])
    c_ref[0] = a * jax.nn.sigmoid(b)


def _proj(x, g1, w_in, qg, kg, bd, layer, aw):
    bsz, seq, d = x.shape
    tm = min(ROW_TILE, seq)
    cw = (w_in.shape[2] - 3 * aw) // 2
    nkt = tm // ATTN_TILE
    return pl.pallas_call(
        functools.partial(_proj_body, aw=aw),
        grid=(bsz, seq // tm),
        in_specs=[
            pl.BlockSpec((1, tm, d), lambda b, s: (b, s, 0)),
            pl.BlockSpec((1, 1, d), lambda b, s: (layer, 0, 0)),
            pl.BlockSpec((1, d, w_in.shape[2]), lambda b, s: (layer, 0, 0),
                         pipeline_mode=pl.Buffered(1)),
            pl.BlockSpec((1, 1, aw), lambda b, s: (layer, 0, 0)),
            pl.BlockSpec((1, 1, aw), lambda b, s: (layer, 0, 0)),
            pl.BlockSpec(bd.shape, lambda b, s: (0, 0)),
        ],
        out_specs=[
            pl.BlockSpec((1, tm, aw), lambda b, s: (b, s, 0)),
            pl.BlockSpec((1, nkt, aw, ATTN_TILE), lambda b, s: (b, s, 0, 0)),
            pl.BlockSpec((1, tm, aw), lambda b, s: (b, s, 0)),
            pl.BlockSpec((1, tm, cw), lambda b, s: (b, s, 0)),
        ],
        out_shape=[
            jax.ShapeDtypeStruct((bsz, seq, aw), BF16),
            jax.ShapeDtypeStruct((bsz, seq // ATTN_TILE, aw, ATTN_TILE), BF16),
            jax.ShapeDtypeStruct((bsz, seq, aw), BF16),
            jax.ShapeDtypeStruct((bsz, seq, cw), F32),
        ],
        compiler_params=_params("parallel", "parallel"),
        name="proj",
    )(x, g1, w_in, qg, kg, bd)


def _attn_body(q_ref, kt_ref, v_ref, u_ref, o_ref):
    t = ATTN_TILE
    qi = pl.program_id(2)
    q = q_ref[0]
    lane = lax.broadcasted_iota(jnp.int32, q.shape, 1)
    zero = jnp.zeros_like(q)
    qh = (jnp.where(lane < HEAD_DIM, q, zero), jnp.where(lane >= HEAD_DIM, q, zero))
    row = lax.broadcasted_iota(jnp.int32, (t, t), 0)
    col = lax.broadcasted_iota(jnp.int32, (t, t), 1)
    causal = col < row
    u = u_ref[...]

    def tile(j, masked, state):
        kt = kt_ref[0, j]
        vv = v_ref[0, pl.ds(pl.multiple_of(j * t, t), t), :]
        out = []
        for h in range(2):
            carry, acc = state[h]
            z = _dot(qh[h], kt)
            lb = jnp.minimum(z, 0.0) - jnp.log(1.0 + jnp.exp(-jnp.abs(z)))
            l1 = lb - z
            if masked:
                l1 = jnp.where(causal, l1, 0.0)
            hi = l1.astype(BF16)
            lo = (l1 - hi.astype(F32)).astype(BF16)
            tail = _dot(hi, u) + _dot(lo, u)
            w = jnp.exp(lb + tail + carry)
            if masked:
                w = jnp.where(causal, w, 0.0)
            acc = acc + _dot(w.astype(BF16), vv)
            carry = carry + jnp.sum(l1, axis=-1, keepdims=True)
            out.append((carry, acc))
        return tuple(out)

    init = tuple((jnp.zeros((t, 1), F32), jnp.zeros((t, LANES), F32)) for _ in range(2))
    state = tile(qi, True, init)
    state = lax.fori_loop(0, qi, lambda i, s: tile(qi - 1 - i, False, s), state)
    o_ref[0] = jnp.where(lane < HEAD_DIM, state[0][1], state[1][1]).astype(o_ref.dtype)


def _attn(q, kt, v, u):
    bsz, seq, aw = q.shape
    t = ATTN_TILE
    return pl.pallas_call(
        _attn_body,
        grid=(bsz, aw // LANES, seq // t),
        in_specs=[
            pl.BlockSpec((1, t, LANES), lambda b, h, i: (b, i, h)),
            pl.BlockSpec((1, seq // t, LANES, t), lambda b, h, i: (b, 0, h, 0)),
            pl.BlockSpec((1, seq, LANES), lambda b, h, i: (b, 0, h)),
            pl.BlockSpec((t, t), lambda b, h, i: (0, 0)),
        ],
        out_specs=pl.BlockSpec((1, t, LANES), lambda b, h, i: (b, i, h)),
        out_shape=jax.ShapeDtypeStruct((bsz, seq, aw), BF16),
        compiler_params=_params("parallel", "parallel", "arbitrary"),
        name="attn",
    )(q, kt, v, u)


def _mix_body(x_ref, a_ref, c_ref, ch_ref, cw_ref, cb_ref, lg_ref, lb_ref, wo_ref, o_ref,
              xp_ref, cs_ref, *, aw):
    tm = x_ref.shape[1]
    kw = cw_ref.shape[1]
    si = pl.program_id(1)
    xp_ref[0:CONV_HALO] = jnp.where(si > 0, ch_ref[0], 0.0)
    xp_ref[CONV_HALO:] = c_ref[0]
    bias = cb_ref[0]
    lng = lg_ref[0]
    lnb = lb_ref[0]
    off = CONV_HALO - (kw - 1)
    for r0 in range(0, tm, CONV_ROWS):
        acc = jnp.broadcast_to(bias, (CONV_ROWS, bias.shape[-1]))
        for k in range(kw):
            acc = acc + cw_ref[0, k:k + 1, :] * xp_ref[r0 + off + k:r0 + off + k + CONV_ROWS, :]
        mu = jnp.mean(acc, axis=-1, keepdims=True)
        xc = acc - mu
        var = jnp.mean(xc * xc, axis=-1, keepdims=True)
        y = xc * lax.rsqrt(var + EPS) * lng + lnb
        cs_ref[r0:r0 + CONV_ROWS] = (y * jax.nn.sigmoid(y)).astype(BF16)
    mixed = _dot(a_ref[0], wo_ref[0, 0:aw, :]) + _dot(cs_ref[...], wo_ref[0, aw:, :])
    o_ref[0] = x_ref[0] + mixed


def _mix(x, attn, c, conv_w, conv_b, ln_g, ln_b, w_out, layer):
    bsz, seq, d = x.shape
    aw = attn.shape[2]
    cw = c.shape[2]
    kw = conv_w.shape[1]
    tm = min(ROW_TILE, seq)
    hb = tm // CONV_HALO
    vec = lambda n: pl.BlockSpec((1, 1, n), lambda b, s: (layer, 0, 0))
    return pl.pallas_call(
        functools.partial(_mix_body, aw=aw),
        grid=(bsz, seq // tm),
        in_specs=[
            pl.BlockSpec((1, tm, d), lambda b, s: (b, s, 0)),
            pl.BlockSpec((1, tm, aw), lambda b, s: (b, s, 0)),
            pl.BlockSpec((1, tm, cw), lambda b, s: (b, s, 0)),
            pl.BlockSpec((1, CONV_HALO, cw), lambda b, s: (b, jnp.maximum(s * hb - 1, 0), 0)),
            pl.BlockSpec((1, kw, cw), lambda b, s: (layer, 0, 0)),
            vec(cw), vec(cw), vec(cw),
            pl.BlockSpec((1, d, d), lambda b, s: (layer, 0, 0), pipeline_mode=pl.Buffered(1)),
        ],
        out_specs=pl.BlockSpec((1, tm, d), lambda b, s: (b, s, 0)),
        out_shape=jax.ShapeDtypeStruct((bsz, seq, d), F32),
        scratch_shapes=[pltpu.VMEM((tm + CONV_HALO, cw), F32), pltpu.VMEM((tm, cw), BF16)],
        compiler_params=_params("parallel", "parallel"),
        name="mix",
    )(x, attn, c, c, conv_w, conv_b, ln_g, ln_b, w_out)


def _ffn_body(x_ref, xh_ref, g2_ref, wu_ref, fw_ref, fb_ref, wd_ref, o_ref, xe_ref, h_ref, acc_ref,
              *, dff):
    tm = x_ref.shape[1]
    si = pl.program_id(1)
    xe_ref[0:tm] = x_ref[0]
    xe_ref[tm:] = jnp.where(si > 0, xh_ref[0], 0.0)
    xe = xe_ref[...]
    ms = jnp.mean(xe * xe, axis=-1, keepdims=True)
    h_ref[...] = (xe * lax.rsqrt(ms + EPS) * g2_ref[0]).astype(BF16)
    acc_ref[...] = jnp.zeros_like(acc_ref)

    def chunk(c, _):
        c0 = pl.multiple_of(c * FFN_CHUNK, FFN_CHUNK)
        g = _dot(h_ref[...], wu_ref[0, :, pl.ds(c0, FFN_CHUNK)])
        val = _dot(h_ref[0:tm], wu_ref[0, :, pl.ds(pl.multiple_of(dff + c0, FFN_CHUNK), FFN_CHUNK)])
        fw = fw_ref[0, :, pl.ds(c0, FFN_CHUNK)]
        conv = (fw[0:1] * pltpu.roll(g, 2, 0)[0:tm] + fw[1:2] * pltpu.roll(g, 1, 0)[0:tm]
                + fw[2:3] * g[0:tm] + fb_ref[0, :, pl.ds(c0, FFN_CHUNK)])
        act = (conv * jax.nn.sigmoid(conv) * val).astype(BF16)
        acc_ref[...] += _dot(act, wd_ref[0, pl.ds(c0, FFN_CHUNK), :])
        return 0

    lax.fori_loop(0, dff // FFN_CHUNK, chunk, 0)
    o_ref[0] = x_ref[0] + acc_ref[...]


def _ffn(x, g2, w_up, ffn_w, ffn_b, w_down, layer):
    bsz, seq, d = x.shape
    dff = w_down.shape[1]
    tm = min(ROW_TILE, seq)
    hb = tm // FFN_HALO
    return pl.pallas_call(
        functools.partial(_ffn_body, dff=dff),
        grid=(bsz, seq // tm),
        in_specs=[
            pl.BlockSpec((1, tm, d), lambda b, s: (b, s, 0)),
            pl.BlockSpec((1, FFN_HALO, d), lambda b, s: (b, jnp.maximum(s * hb - 1, 0), 0)),
            pl.BlockSpec((1, 1, d), lambda b, s: (layer, 0, 0)),
            pl.BlockSpec((1, d, 2 * dff), lambda b, s: (layer, 0, 0), pipeline_mode=pl.Buffered(1)),
            pl.BlockSpec((1, ffn_w.shape[1], dff), lambda b, s: (layer, 0, 0)),
            pl.BlockSpec((1, 1, dff), lambda b, s: (layer, 0, 0)),
            pl.BlockSpec((1, dff, d), lambda b, s: (layer, 0, 0), pipeline_mode=pl.Buffered(1)),
        ],
        out_specs=pl.BlockSpec((1, tm, d), lambda b, s: (b, s, 0)),
        out_shape=jax.ShapeDtypeStruct((bsz, seq, d), F32),
        scratch_shapes=[
            pltpu.VMEM((tm + FFN_HALO, d), F32),
            pltpu.VMEM((tm + FFN_HALO, d), BF16),
            pltpu.VMEM((tm, d), F32),
        ],
        compiler_params=_params("parallel", "parallel"),
        name="ffn",
    )(x, x, g2, w_up, ffn_w, ffn_b, w_down)


def kernel(x, norm1_g, w_in, q_norm_g, k_norm_g, conv_dw_w, conv_dw_b, conv_ln_g, conv_ln_b,
           w_out, norm2_g, w_up, ffn_dw_w, ffn_dw_b, w_down):
    depth = w_in.shape[0]
    aw = w_out.shape[1] - conv_dw_w.shape[2]
    heads = aw // HEAD_DIM
    row3 = lambda p: p[:, None, :]

    w_in_b, w_out_b, w_up_b, w_down_b = (w.astype(BF16) for w in (w_in, w_out, w_up, w_down))
    qg = row3(jnp.tile(q_norm_g, (1, heads)))
    kg = row3(jnp.tile(k_norm_g, (1, heads)))
    g1, g2 = row3(norm1_g), row3(norm2_g)
    cb, lg, lb, fb = row3(conv_dw_b), row3(conv_ln_g), row3(conv_ln_b), row3(ffn_dw_b)

    idx = jnp.arange(2 * LANES)
    bd = (idx[:, None] // HEAD_DIM == idx[None, :] // HEAD_DIM).astype(BF16)
    tix = jnp.arange(ATTN_TILE)
    u = (tix[:, None] > tix[None, :]).astype(BF16)

    for layer in range(depth):
        q, kt, v, c = _proj(x, g1, w_in_b, qg, kg, bd, layer, aw)
        attn = _attn(q, kt, v, u)
        x = _mix(x, attn, c, conv_dw_w, cb, lg, lb, w_out_b, layer)
        x = _ffn(x, g2, w_up_b, ffn_dw_w, fb, w_down_b, layer)
    return x
```

```python
import functools

import jax
import jax.numpy as jnp
from jax import lax
from jax.experimental import pallas as pl
from jax.experimental.pallas import tpu as pltpu

F32 = jnp.float32
BF16 = jnp.bfloat16

EPS = 1e-6
HEAD_DIM = 64
LANES = 128
ATTN_TILE = 256
ROW_TILE = 512
CONV_HALO = 32
FFN_HALO = 16
FFN_CHUNK = 256
CONV_ROWS = 32
VMEM_LIMIT = 56 * 1024 * 1024
MASK_BIAS = -1e30
LOG2E = 1.4426950408889634


def _params(*sem):
    return pltpu.CompilerParams(dimension_semantics=sem, vmem_limit_bytes=VMEM_LIMIT)


def _dot(a, b):
    return jnp.dot(a, b, preferred_element_type=F32)


def _proj_body(x_ref, g1_ref, w_ref, qg_ref, kg_ref, bd_ref, q_ref, kt_ref, v_ref, c_ref, *, aw):
    x = x_ref[0]
    ms = jnp.mean(x * x, axis=-1, keepdims=True)
    h = (x * lax.rsqrt(ms + EPS) * g1_ref[0]).astype(BF16)

    def head_norm(t, g):
        sq = (t * t).astype(BF16)
        half = 2 * LANES
        ss = jnp.concatenate(
            [_dot(sq[:, i * half:(i + 1) * half], bd_ref[...]) for i in range(aw // half)], axis=-1)
        return t * lax.rsqrt(ss * (1.0 / HEAD_DIM) + EPS) * g

    q = head_norm(_dot(h, w_ref[0, :, 0:aw]), qg_ref[0])
    q = (q * (HEAD_DIM ** -0.5 * LOG2E)).astype(BF16)
    k = head_norm(_dot(h, w_ref[0, :, aw:2 * aw]), kg_ref[0])
    for i in range(k.shape[0] // ATTN_TILE):
        kt_ref[0, i] = k[i * ATTN_TILE:(i + 1) * ATTN_TILE, :].T.astype(BF16)
    v = _dot(h, w_ref[0, :, 2 * aw:3 * aw]).astype(BF16)
    for p in range(aw // LANES):
        q_ref[0, p] = q[:, p * LANES:(p + 1) * LANES]
        v_ref[0, p] = v[:, p * LANES:(p + 1) * LANES]
    cw = (w_ref.shape[2] - 3 * aw) // 2
    a = _dot(h, w_ref[0, :, 3 * aw:3 * aw + cw])
    b = _dot(h, w_ref[0, :, 3 * aw + cw:3 * aw + 2 * cw])
    c_ref[0] = a * jax.nn.sigmoid(b)


def _proj(x, g1, w_in, qg, kg, bd, layer, aw):
    bsz, seq, d = x.shape
    tm = min(ROW_TILE, seq)
    cw = (w_in.shape[2] - 3 * aw) // 2
    nkt = tm // ATTN_TILE
    nhp = aw // LANES
    return pl.pallas_call(
        functools.partial(_proj_body, aw=aw),
        grid=(bsz, seq // tm),
        in_specs=[
            pl.BlockSpec((1, tm, d), lambda b, s: (b, s, 0)),
            pl.BlockSpec((1, 1, d), lambda b, s: (layer, 0, 0)),
            pl.BlockSpec((1, d, w_in.shape[2]), lambda b, s: (layer, 0, 0),
                         pipeline_mode=pl.Buffered(1)),
            pl.BlockSpec((1, 1, aw), lambda b, s: (layer, 0, 0)),
            pl.BlockSpec((1, 1, aw), lambda b, s: (layer, 0, 0)),
            pl.BlockSpec(bd.shape, lambda b, s: (0, 0)),
        ],
        out_specs=[
            pl.BlockSpec((1, nhp, tm, LANES), lambda b, s: (b, 0, s, 0)),
            pl.BlockSpec((1, nkt, aw, ATTN_TILE), lambda b, s: (b, s, 0, 0)),
            pl.BlockSpec((1, nhp, tm, LANES), lambda b, s: (b, 0, s, 0)),
            pl.BlockSpec((1, tm, cw), lambda b, s: (b, s, 0)),
        ],
        out_shape=[
            jax.ShapeDtypeStruct((bsz, nhp, seq, LANES), BF16),
            jax.ShapeDtypeStruct((bsz, seq // ATTN_TILE, aw, ATTN_TILE), BF16),
            jax.ShapeDtypeStruct((bsz, nhp, seq, LANES), BF16),
            jax.ShapeDtypeStruct((bsz, seq, cw), F32),
        ],
        compiler_params=_params("parallel", "parallel"),
        name="proj",
    )(x, g1, w_in, qg, kg, bd)


def _attn_body(q_ref, kt_ref, v_ref, u_ref, bias_ref, o_ref, zs_ref, hi_ref, xb_ref, ts_ref, w_ref,
               acc_ref, carry_ref, *, nq, nhp):
    t = ATTN_TILE
    lane = lax.broadcasted_iota(jnp.int32, (t, LANES), 1)
    low_head = lane < HEAD_DIM
    for ref in (zs_ref, hi_ref, ts_ref, w_ref, acc_ref, carry_ref):
        ref[...] = jnp.zeros_like(ref)
    xb_ref[...] = jnp.full(xb_ref.shape, MASK_BIAS, F32)

    def advance(meta):
        hp, qi, j = meta
        row_done = j == 0
        qn = jnp.where(row_done, qi + 1, qi)
        jn = jnp.where(row_done, qi + 1, j - 1)
        pair_done = qn >= nq
        hn = jnp.where(pair_done, hp + 1, hp)
        qn = jnp.where(pair_done, 0, qn)
        jn = jnp.where(pair_done, 0, jn)
        over = hn >= nhp
        return jnp.where(over, nhp - 1, hn), jnp.where(over, nq - 1, qn), jnp.where(over, 0, jn)

    def substep(p, metas):
        m1, e1, _, _, m3 = metas
        hp, qi, j = m3
        pv = _dot(w_ref[p], v_ref[0, hp, pl.ds(pl.multiple_of(j * t, t), t), :])
        acc = jnp.where(j == qi, pv, acc_ref[...] + pv)
        acc_ref[...] = acc
        o_ref[0, hp, pl.ds(pl.multiple_of(qi * t, t), t), :] = (
            jnp.where(low_head, acc[0:t], acc[t:2 * t]).astype(o_ref.dtype))
        w_ref[1 - p] = jnp.exp2(xb_ref[1 - p] + ts_ref[1 - p]).astype(BF16)
        ts_ref[p] = _dot(hi_ref[p], u_ref[...])
        diag = e1[2] == e1[1]
        z = zs_ref[1 - p] + bias_ref[diag.astype(jnp.int32)]
        neg_abs = lax.bitcast_convert_type(
            lax.bitcast_convert_type(z, jnp.uint32) | jnp.uint32(0x80000000), F32)
        lb = jnp.minimum(z, 0.0) - jnp.log2(1.0 + jnp.exp2(neg_abs))
        l1 = lb - z
        hi_ref[1 - p] = l1.astype(BF16)
        cin = jnp.where(diag, 0.0, carry_ref[...])
        xb_ref[1 - p] = lb + jnp.concatenate([cin] * (t // LANES), axis=-1)
        carry_ref[...] = cin + jnp.sum(l1, axis=-1, keepdims=True)
        hp, qi, j = m1
        q = q_ref[0, hp, pl.ds(pl.multiple_of(qi * t, t), t), :]
        zero = jnp.zeros_like(q)
        qq = jnp.concatenate([jnp.where(low_head, q, zero), jnp.where(low_head, zero, q)], axis=0)
        zs_ref[p] = _dot(qq, kt_ref[0, j, hp])
        return advance(m1), m1, e1, metas[2], metas[3]

    def step(_, metas):
        return substep(1, substep(0, metas))

    n_tiles = nhp * (nq * (nq + 1) // 2)
    assert n_tiles % 2 == 0
    m0 = (jnp.int32(0), jnp.int32(0), jnp.int32(0))
    lax.fori_loop(0, (n_tiles + 4) // 2, step, (m0,) * 5)


def _attn(q, kt, v, u, bias):
    bsz, nhp, seq, _ = q.shape
    t = ATTN_TILE
    nq = seq // t
    whole = lambda shape: pl.BlockSpec((1,) + shape, lambda b: (b,) + (0,) * len(shape))
    pair = lambda dt: pltpu.VMEM((2, 2 * t, t), dt)
    return pl.pallas_call(
        functools.partial(_attn_body, nq=nq, nhp=nhp),
        grid=(bsz,),
        in_specs=[
            whole((nhp, seq, LANES)),
            whole((nq, nhp, LANES, t)),
            whole((nhp, seq, LANES)),
            pl.BlockSpec((t, t), lambda b: (0, 0)),
            pl.BlockSpec((2, 2 * t, t), lambda b: (0, 0, 0)),
        ],
        out_specs=whole((nhp, seq, LANES)),
        out_shape=jax.ShapeDtypeStruct((bsz, nhp, seq, LANES), BF16),
        scratch_shapes=[pair(F32), pair(BF16), pair(F32), pair(F32), pair(BF16),
                        pltpu.VMEM((2 * t, LANES), F32), pltpu.VMEM((2 * t, LANES), F32)],
        compiler_params=_params("parallel"),
        name="attn",
    )(q, kt.reshape(bsz, nq, nhp, LANES, t), v, u, bias)


def _mix_body(x_ref, a_ref, c_ref, ch_ref, cw_ref, cb_ref, lg_ref, lb_ref, wo_ref, o_ref,
              xp_ref, cs_ref, *, aw):
    tm = x_ref.shape[1]
    kw = cw_ref.shape[1]
    si = pl.program_id(1)
    ngrp = xp_ref.shape[0]
    for g in range(ngrp):
        lanes = slice(g * LANES, (g + 1) * LANES)
        xp_ref[g, 0:CONV_HALO] = jnp.where(si > 0, ch_ref[0, :, lanes], 0.0)
        xp_ref[g, CONV_HALO:] = c_ref[0, :, lanes]
    bias = cb_ref[0]
    lng = lg_ref[0]
    lnb = lb_ref[0]
    off = CONV_HALO - (kw - 1)
    for r0 in range(0, tm, CONV_ROWS):
        parts = []
        for g in range(ngrp):
            lanes = slice(g * LANES, (g + 1) * LANES)
            part = jnp.broadcast_to(bias[:, lanes], (CONV_ROWS, LANES))
            for k in range(kw):
                part = part + cw_ref[0, k:k + 1, lanes] * xp_ref[g, r0 + off + k:r0 + off + k + CONV_ROWS, :]
            parts.append(part)
        acc = jnp.concatenate(parts, axis=-1)
        mu = jnp.mean(acc, axis=-1, keepdims=True)
        xc = acc - mu
        var = jnp.mean(xc * xc, axis=-1, keepdims=True)
        y = xc * lax.rsqrt(var + EPS) * lng + lnb
        cs_ref[r0:r0 + CONV_ROWS] = (y * jax.nn.sigmoid(y)).astype(BF16)
    attn = jnp.concatenate([a_ref[0, p] for p in range(a_ref.shape[1])], axis=-1)
    mixed = _dot(attn, wo_ref[0, 0:aw, :]) + _dot(cs_ref[...], wo_ref[0, aw:, :])
    o_ref[0] = x_ref[0] + mixed


def _mix(x, attn, c, conv_w, conv_b, ln_g, ln_b, w_out, layer):
    bsz, seq, d = x.shape
    nhp = attn.shape[1]
    aw = nhp * LANES
    cw = c.shape[2]
    kw = conv_w.shape[1]
    tm = min(ROW_TILE, seq)
    hb = tm // CONV_HALO
    vec = lambda n: pl.BlockSpec((1, 1, n), lambda b, s: (layer, 0, 0))
    return pl.pallas_call(
        functools.partial(_mix_body, aw=aw),
        grid=(bsz, seq // tm),
        in_specs=[
            pl.BlockSpec((1, tm, d), lambda b, s: (b, s, 0)),
            pl.BlockSpec((1, nhp, tm, LANES), lambda b, s: (b, 0, s, 0)),
            pl.BlockSpec((1, tm, cw), lambda b, s: (b, s, 0)),
            pl.BlockSpec((1, CONV_HALO, cw), lambda b, s: (b, jnp.maximum(s * hb - 1, 0), 0)),
            pl.BlockSpec((1, kw, cw), lambda b, s: (layer, 0, 0)),
            vec(cw), vec(cw), vec(cw),
            pl.BlockSpec((1, d, d), lambda b, s: (layer, 0, 0), pipeline_mode=pl.Buffered(1)),
        ],
        out_specs=pl.BlockSpec((1, tm, d), lambda b, s: (b, s, 0)),
        out_shape=jax.ShapeDtypeStruct((bsz, seq, d), F32),
        scratch_shapes=[pltpu.VMEM((cw // LANES, tm + CONV_HALO, LANES), F32),
                        pltpu.VMEM((tm, cw), BF16)],
        compiler_params=_params("parallel", "parallel"),
        name="mix",
    )(x, attn, c, c, conv_w, conv_b, ln_g, ln_b, w_out)


def _ffn_body(x_ref, xh_ref, g2_ref, wu_ref, fw_ref, fb_ref, wd_ref, o_ref, xe_ref, h_ref, acc_ref,
              graw_ref, vraw_ref, act_ref, *, dff):
    tm = x_ref.shape[1]
    n = FFN_CHUNK
    nchunks = dff // n
    si = pl.program_id(1)
    xe_ref[0:tm] = x_ref[0]
    xe_ref[tm:] = jnp.where(si > 0, xh_ref[0], 0.0)
    xe = xe_ref[...]
    ms = jnp.mean(xe * xe, axis=-1, keepdims=True)
    h_ref[...] = (xe * lax.rsqrt(ms + EPS) * g2_ref[0]).astype(BF16)
    acc_ref[...] = jnp.zeros_like(acc_ref)

    def col(c, base=0):
        return base + c * n if isinstance(c, int) else pl.multiple_of(base + c * n, n)

    def step(c, p, up=True, gate=True, down=True):
        if gate:
            g = graw_ref[1 - p]
            fw = fw_ref[0, :, pl.ds(col(c - 1), n)]
            conv = (fw[0:1] * pltpu.roll(g, 2, 0)[0:tm] + fw[1:2] * pltpu.roll(g, 1, 0)[0:tm]
                    + fw[2:3] * g[0:tm] + fb_ref[0, :, pl.ds(col(c - 1), n)])
            act_ref[1 - p] = (conv * jax.nn.sigmoid(conv) * vraw_ref[1 - p]).astype(BF16)
        if down:
            acc_ref[...] += _dot(act_ref[p], wd_ref[0, pl.ds(col(c - 2), n), :])
        if up:
            graw_ref[p] = _dot(h_ref[...], wu_ref[0, :, pl.ds(col(c), n)])
            vraw_ref[p] = _dot(h_ref[0:tm], wu_ref[0, :, pl.ds(col(c, dff), n)])

    step(0, 0, gate=False, down=False)
    step(1, 1, down=False)
    pairs = (nchunks - 2) // 2

    def body(i, _):
        step(2 + 2 * i, 0)
        step(3 + 2 * i, 1)
        return 0

    lax.fori_loop(0, pairs, body, 0)
    for c in range(2 + 2 * pairs, nchunks + 2):
        step(c, c % 2, up=c < nchunks, gate=c - 1 < nchunks)
    o_ref[0] = x_ref[0] + acc_ref[...]


def _ffn(x, g2, w_up, ffn_w, ffn_b, w_down, layer):
    bsz, seq, d = x.shape
    dff = w_down.shape[1]
    tm = min(ROW_TILE, seq)
    hb = tm // FFN_HALO
    return pl.pallas_call(
        functools.partial(_ffn_body, dff=dff),
        grid=(bsz, seq // tm),
        in_specs=[
            pl.BlockSpec((1, tm, d), lambda b, s: (b, s, 0)),
            pl.BlockSpec((1, FFN_HALO, d), lambda b, s: (b, jnp.maximum(s * hb - 1, 0), 0)),
            pl.BlockSpec((1, 1, d), lambda b, s: (layer, 0, 0)),
            pl.BlockSpec((1, d, 2 * dff), lambda b, s: (layer, 0, 0), pipeline_mode=pl.Buffered(1)),
            pl.BlockSpec((1, ffn_w.shape[1], dff), lambda b, s: (layer, 0, 0)),
            pl.BlockSpec((1, 1, dff), lambda b, s: (layer, 0, 0)),
            pl.BlockSpec((1, dff, d), lambda b, s: (layer, 0, 0), pipeline_mode=pl.Buffered(1)),
        ],
        out_specs=pl.BlockSpec((1, tm, d), lambda b, s: (b, s, 0)),
        out_shape=jax.ShapeDtypeStruct((bsz, seq, d), F32),
        scratch_shapes=[
            pltpu.VMEM((tm + FFN_HALO, d), F32),
            pltpu.VMEM((tm + FFN_HALO, d), BF16),
            pltpu.VMEM((tm, d), F32),
            pltpu.VMEM((2, tm + FFN_HALO, FFN_CHUNK), F32),
            pltpu.VMEM((2, tm, FFN_CHUNK), F32),
            pltpu.VMEM((2, tm, FFN_CHUNK), BF16),
        ],
        compiler_params=_params("parallel", "parallel"),
        name="ffn",
    )(x, x, g2, w_up, ffn_w, ffn_b, w_down)


def kernel(x, norm1_g, w_in, q_norm_g, k_norm_g, conv_dw_w, conv_dw_b, conv_ln_g, conv_ln_b,
           w_out, norm2_g, w_up, ffn_dw_w, ffn_dw_b, w_down):
    depth = w_in.shape[0]
    aw = w_out.shape[1] - conv_dw_w.shape[2]
    heads = aw // HEAD_DIM
    row3 = lambda p: p[:, None, :]

    w_in_b, w_out_b, w_up_b, w_down_b = (w.astype(BF16) for w in (w_in, w_out, w_up, w_down))
    qg = row3(jnp.tile(q_norm_g, (1, heads)))
    kg = row3(jnp.tile(k_norm_g, (1, heads)))
    g1, g2 = row3(norm1_g), row3(norm2_g)
    cb, lg, lb, fb = row3(conv_dw_b), row3(conv_ln_g), row3(conv_ln_b), row3(ffn_dw_b)

    idx = jnp.arange(2 * LANES)
    bd = (idx[:, None] // HEAD_DIM == idx[None, :] // HEAD_DIM).astype(BF16)
    tix = jnp.arange(ATTN_TILE)
    u = (tix[:, None] > tix[None, :]).astype(BF16)
    causal_bias = jnp.where(tix[None, :] < tix[:, None], 0.0, MASK_BIAS).astype(F32)
    causal_bias = jnp.concatenate([causal_bias, causal_bias], axis=0)
    bias = jnp.stack([jnp.zeros_like(causal_bias), causal_bias])

    for layer in range(depth):
        q, kt, v, c = _proj(x, g1, w_in_b, qg, kg, bd, layer, aw)
        attn = _attn(q, kt, v, u, bias)
        x = _mix(x, attn, c, conv_dw_w, cb, lg, lb, w_out_b, layer)
        x = _ffn(x, g2, w_up_b, ffn_dw_w, fb, w_down_b, layer)
    return x
```

```python
import functools

import jax
import jax.numpy as jnp
from jax import lax
from jax.experimental import pallas as pl
from jax.experimental.pallas import tpu as pltpu

F32 = jnp.float32
BF16 = jnp.bfloat16

EPS = 1e-6
HEAD_DIM = 64
LANES = 128
ATTN_TILE = 256
ROW_TILE = 512
CONV_HALO = 32
FFN_HALO = 16
FFN_CHUNK = 256
CONV_ROWS = 32
VMEM_LIMIT = 56 * 1024 * 1024
MASK_BIAS = -1e30
LOG2E = 1.4426950408889634


def _params(*sem):
    return pltpu.CompilerParams(dimension_semantics=sem, vmem_limit_bytes=VMEM_LIMIT)


def _dot(a, b):
    return jnp.dot(a, b, preferred_element_type=F32)


def _proj_body(x_ref, g1_ref, w_ref, qg_ref, kg_ref, bd_ref, q_ref, kt_ref, v_ref, c_ref, *, aw):
    x = x_ref[0]
    ms = jnp.mean(x * x, axis=-1, keepdims=True)
    h = (x * lax.rsqrt(ms + EPS) * g1_ref[0]).astype(BF16)

    def head_norm(t, g):
        sq = (t * t).astype(BF16)
        half = 2 * LANES
        ss = jnp.concatenate(
            [_dot(sq[:, i * half:(i + 1) * half], bd_ref[...]) for i in range(aw // half)], axis=-1)
        return t * lax.rsqrt(ss * (1.0 / HEAD_DIM) + EPS) * g

    q = head_norm(_dot(h, w_ref[0, :, 0:aw]), qg_ref[0])
    q = (q * (HEAD_DIM ** -0.5 * LOG2E)).astype(BF16)
    k = head_norm(_dot(h, w_ref[0, :, aw:2 * aw]), kg_ref[0])
    for i in range(k.shape[0] // ATTN_TILE):
        kt_ref[0, i] = k[i * ATTN_TILE:(i + 1) * ATTN_TILE, :].T.astype(BF16)
    v = _dot(h, w_ref[0, :, 2 * aw:3 * aw]).astype(BF16)
    for p in range(aw // LANES):
        q_ref[0, p] = q[:, p * LANES:(p + 1) * LANES]
        v_ref[0, p] = v[:, p * LANES:(p + 1) * LANES]
    cw = (w_ref.shape[2] - 3 * aw) // 2
    a = _dot(h, w_ref[0, :, 3 * aw:3 * aw + cw])
    b = _dot(h, w_ref[0, :, 3 * aw + cw:3 * aw + 2 * cw])
    c_ref[0] = a * jax.nn.sigmoid(b)


def _proj(x, g1, w_in, qg, kg, bd, layer, aw):
    bsz, seq, d = x.shape
    tm = min(ROW_TILE, seq)
    cw = (w_in.shape[2] - 3 * aw) // 2
    nkt = tm // ATTN_TILE
    nhp = aw // LANES
    return pl.pallas_call(
        functools.partial(_proj_body, aw=aw),
        grid=(bsz, seq // tm),
        in_specs=[
            pl.BlockSpec((1, tm, d), lambda b, s: (b, s, 0)),
            pl.BlockSpec((1, 1, d), lambda b, s: (layer, 0, 0)),
            pl.BlockSpec((1, d, w_in.shape[2]), lambda b, s: (layer, 0, 0),
                         pipeline_mode=pl.Buffered(1)),
            pl.BlockSpec((1, 1, aw), lambda b, s: (layer, 0, 0)),
            pl.BlockSpec((1, 1, aw), lambda b, s: (layer, 0, 0)),
            pl.BlockSpec(bd.shape, lambda b, s: (0, 0)),
        ],
        out_specs=[
            pl.BlockSpec((1, nhp, tm, LANES), lambda b, s: (b, 0, s, 0)),
            pl.BlockSpec((1, nkt, aw, ATTN_TILE), lambda b, s: (b, s, 0, 0)),
            pl.BlockSpec((1, nhp, tm, LANES), lambda b, s: (b, 0, s, 0)),
            pl.BlockSpec((1, tm, cw), lambda b, s: (b, s, 0)),
        ],
        out_shape=[
            jax.ShapeDtypeStruct((bsz, nhp, seq, LANES), BF16),
            jax.ShapeDtypeStruct((bsz, seq // ATTN_TILE, aw, ATTN_TILE), BF16),
            jax.ShapeDtypeStruct((bsz, nhp, seq, LANES), BF16),
            jax.ShapeDtypeStruct((bsz, seq, cw), F32),
        ],
        compiler_params=_params("parallel", "parallel"),
        name="proj",
    )(x, g1, w_in, qg, kg, bd)


def _attn_body(q_ref, kt_ref, v_ref, u_ref, bias_ref, o_ref, zs_ref, hi_ref, xb_ref, ts_ref, w_ref,
               acc_ref, carry_ref, *, nq, nhp):
    t = ATTN_TILE
    lane = lax.broadcasted_iota(jnp.int32, (t, LANES), 1)
    low_head = lane < HEAD_DIM
    for ref in (zs_ref, hi_ref, ts_ref, w_ref, acc_ref, carry_ref):
        ref[...] = jnp.zeros_like(ref)
    xb_ref[...] = jnp.full(xb_ref.shape, MASK_BIAS, F32)

    def advance(meta):
        hp, qi, j = meta
        row_done = j == 0
        qn = jnp.where(row_done, qi + 1, qi)
        jn = jnp.where(row_done, qi + 1, j - 1)
        pair_done = qn >= nq
        hn = jnp.where(pair_done, hp + 1, hp)
        qn = jnp.where(pair_done, 0, qn)
        jn = jnp.where(pair_done, 0, jn)
        over = hn >= nhp
        return jnp.where(over, nhp - 1, hn), jnp.where(over, nq - 1, qn), jnp.where(over, 0, jn)

    def substep(p, metas):
        m1, e1, _, _, m3 = metas
        hp, qi, j = m3
        pv = _dot(w_ref[p], v_ref[0, hp, pl.ds(pl.multiple_of(j * t, t), t), :])
        acc = jnp.where(j == qi, pv, acc_ref[...] + pv)
        acc_ref[...] = acc
        o_ref[0, hp, pl.ds(pl.multiple_of(qi * t, t), t), :] = (
            jnp.where(low_head, acc[0:t], acc[t:2 * t]).astype(o_ref.dtype))
        w_ref[1 - p] = jnp.exp2(xb_ref[1 - p] + ts_ref[1 - p]).astype(BF16)
        ts_ref[p] = _dot(hi_ref[p], u_ref[...])
        diag = e1[2] == e1[1]
        z = zs_ref[1 - p] + bias_ref[diag.astype(jnp.int32)]
        neg_abs = lax.bitcast_convert_type(
            lax.bitcast_convert_type(z, jnp.uint32) | jnp.uint32(0x80000000), F32)
        lb = jnp.minimum(z, 0.0) - jnp.log2(1.0 + jnp.exp2(neg_abs))
        l1 = lb - z
        hi_ref[1 - p] = l1.astype(BF16)
        cin = jnp.where(diag, 0.0, carry_ref[...])
        xb_ref[1 - p] = lb + jnp.concatenate([cin] * (t // LANES), axis=-1)
        carry_ref[...] = cin + jnp.sum(l1, axis=-1, keepdims=True)
        hp, qi, j = m1
        q = q_ref[0, hp, pl.ds(pl.multiple_of(qi * t, t), t), :]
        zero = jnp.zeros_like(q)
        qq = jnp.concatenate([jnp.where(low_head, q, zero), jnp.where(low_head, zero, q)], axis=0)
        zs_ref[p] = _dot(qq, kt_ref[0, j, hp])
        return advance(m1), m1, e1, metas[2], metas[3]

    def step(_, metas):
        return substep(1, substep(0, substep(1, substep(0, metas))))

    n_tiles = nhp * (nq * (nq + 1) // 2)
    assert (n_tiles + 4) % 4 == 0
    m0 = (jnp.int32(0), jnp.int32(0), jnp.int32(0))
    lax.fori_loop(0, (n_tiles + 4) // 4, step, (m0,) * 5)


def _attn(q, kt, v, u, bias):
    bsz, nhp, seq, _ = q.shape
    t = ATTN_TILE
    nq = seq // t
    whole = lambda shape: pl.BlockSpec((1,) + shape, lambda b: (b,) + (0,) * len(shape))
    pair = lambda dt: pltpu.VMEM((2, 2 * t, t), dt)
    return pl.pallas_call(
        functools.partial(_attn_body, nq=nq, nhp=nhp),
        grid=(bsz,),
        in_specs=[
            whole((nhp, seq, LANES)),
            whole((nq, nhp, LANES, t)),
            whole((nhp, seq, LANES)),
            pl.BlockSpec((t, t), lambda b: (0, 0)),
            pl.BlockSpec((2, 2 * t, t), lambda b: (0, 0, 0)),
        ],
        out_specs=whole((nhp, seq, LANES)),
        out_shape=jax.ShapeDtypeStruct((bsz, nhp, seq, LANES), BF16),
        scratch_shapes=[pair(F32), pair(BF16), pair(F32), pair(F32), pair(BF16),
                        pltpu.VMEM((2 * t, LANES), F32), pltpu.VMEM((2 * t, LANES), F32)],
        compiler_params=_params("parallel"),
        name="attn",
    )(q, kt.reshape(bsz, nq, nhp, LANES, t), v, u, bias)


def _mix_body(x_ref, a_ref, c_ref, ch_ref, cw_ref, cb_ref, lg_ref, lb_ref, wo_ref, o_ref,
              xp_ref, cs_ref, *, aw):
    tm = x_ref.shape[1]
    kw = cw_ref.shape[1]
    si = pl.program_id(1)
    ngrp = xp_ref.shape[0]
    for g in range(ngrp):
        lanes = slice(g * LANES, (g + 1) * LANES)
        xp_ref[g, 0:CONV_HALO] = jnp.where(si > 0, ch_ref[0, :, lanes], 0.0)
        xp_ref[g, CONV_HALO:] = c_ref[0, :, lanes]
    bias = cb_ref[0]
    lng = lg_ref[0]
    lnb = lb_ref[0]
    off = CONV_HALO - (kw - 1)
    for r0 in range(0, tm, CONV_ROWS):
        parts = []
        for g in range(ngrp):
            lanes = slice(g * LANES, (g + 1) * LANES)
            part = jnp.broadcast_to(bias[:, lanes], (CONV_ROWS, LANES))
            for k in range(kw):
                part = part + cw_ref[0, k:k + 1, lanes] * xp_ref[g, r0 + off + k:r0 + off + k + CONV_ROWS, :]
            parts.append(part)
        acc = jnp.concatenate(parts, axis=-1)
        mu = jnp.mean(acc, axis=-1, keepdims=True)
        xc = acc - mu
        var = jnp.mean(xc * xc, axis=-1, keepdims=True)
        y = xc * lax.rsqrt(var + EPS) * lng + lnb
        cs_ref[r0:r0 + CONV_ROWS] = (y * jax.nn.sigmoid(y)).astype(BF16)
    attn = jnp.concatenate([a_ref[0, p] for p in range(a_ref.shape[1])], axis=-1)
    mixed = _dot(attn, wo_ref[0, 0:aw, :]) + _dot(cs_ref[...], wo_ref[0, aw:, :])
    o_ref[0] = x_ref[0] + mixed


def _mix(x, attn, c, conv_w, conv_b, ln_g, ln_b, w_out, layer):
    bsz, seq, d = x.shape
    nhp = attn.shape[1]
    aw = nhp * LANES
    cw = c.shape[2]
    kw = conv_w.shape[1]
    tm = min(ROW_TILE, seq)
    hb = tm // CONV_HALO
    vec = lambda n: pl.BlockSpec((1, 1, n), lambda b, s: (layer, 0, 0))
    return pl.pallas_call(
        functools.partial(_mix_body, aw=aw),
        grid=(bsz, seq // tm),
        in_specs=[
            pl.BlockSpec((1, tm, d), lambda b, s: (b, s, 0)),
            pl.BlockSpec((1, nhp, tm, LANES), lambda b, s: (b, 0, s, 0)),
            pl.BlockSpec((1, tm, cw), lambda b, s: (b, s, 0)),
            pl.BlockSpec((1, CONV_HALO, cw), lambda b, s: (b, jnp.maximum(s * hb - 1, 0), 0)),
            pl.BlockSpec((1, kw, cw), lambda b, s: (layer, 0, 0)),
            vec(cw), vec(cw), vec(cw),
            pl.BlockSpec((1, d, d), lambda b, s: (layer, 0, 0), pipeline_mode=pl.Buffered(1)),
        ],
        out_specs=pl.BlockSpec((1, tm, d), lambda b, s: (b, s, 0)),
        out_shape=jax.ShapeDtypeStruct((bsz, seq, d), F32),
        scratch_shapes=[pltpu.VMEM((cw // LANES, tm + CONV_HALO, LANES), F32),
                        pltpu.VMEM((tm, cw), BF16)],
        compiler_params=_params("parallel", "parallel"),
        name="mix",
    )(x, attn, c, c, conv_w, conv_b, ln_g, ln_b, w_out)


def _ffn_body(x_ref, xh_ref, g2_ref, wu_ref, fw_ref, fb_ref, wd_ref, o_ref, xe_ref, h_ref, acc_ref,
              graw_ref, vraw_ref, act_ref, *, dff):
    tm = x_ref.shape[1]
    n = FFN_CHUNK
    nchunks = dff // n
    si = pl.program_id(1)
    xe_ref[0:FFN_HALO] = jnp.where(si > 0, xh_ref[0], 0.0)
    xe_ref[FFN_HALO:] = x_ref[0]
    xe = xe_ref[...]
    ms = jnp.mean(xe * xe, axis=-1, keepdims=True)
    h_ref[...] = (xe * lax.rsqrt(ms + EPS) * g2_ref[0]).astype(BF16)
    acc_ref[...] = jnp.zeros_like(acc_ref)

    def col(c, base=0):
        return base + c * n if isinstance(c, int) else pl.multiple_of(base + c * n, n)

    def step(c, p, up=True, gate=True, down=True):
        if gate:
            fw = fw_ref[0, :, pl.ds(col(c - 1), n)]
            fb = fb_ref[0, :, pl.ds(col(c - 1), n)]
            taps = fw.shape[0]
            for lg in range(n // LANES):
                lanes = slice(lg * LANES, (lg + 1) * LANES)
                conv = fb[:, lanes]
                for k in range(taps):
                    r0 = FFN_HALO - (taps - 1 - k)
                    conv = conv + fw[k:k + 1, lanes] * graw_ref[1 - p, lg, r0:r0 + tm, :]
                act_ref[1 - p, :, lanes] = (
                    conv * jax.nn.sigmoid(conv) * vraw_ref[1 - p, :, lanes]).astype(BF16)
        if down:
            acc_ref[...] += _dot(act_ref[p], wd_ref[0, pl.ds(col(c - 2), n), :])
        if up:
            g = _dot(h_ref[...], wu_ref[0, :, pl.ds(col(c), n)])
            for lg in range(n // LANES):
                graw_ref[p, lg] = g[:, lg * LANES:(lg + 1) * LANES]
            vraw_ref[p] = _dot(h_ref[FFN_HALO:], wu_ref[0, :, pl.ds(col(c, dff), n)])

    step(0, 0, gate=False, down=False)
    step(1, 1, down=False)
    quads = (nchunks - 2) // 4

    def body(i, _):
        for k in range(4):
            step(2 + 4 * i + k, k % 2)
        return 0

    lax.fori_loop(0, quads, body, 0)
    for c in range(2 + 4 * quads, nchunks + 2):
        step(c, c % 2, up=c < nchunks, gate=c - 1 < nchunks)
    o_ref[0] = x_ref[0] + acc_ref[...]


def _ffn(x, g2, w_up, ffn_w, ffn_b, w_down, layer):
    bsz, seq, d = x.shape
    dff = w_down.shape[1]
    tm = min(ROW_TILE, seq)
    hb = tm // FFN_HALO
    return pl.pallas_call(
        functools.partial(_ffn_body, dff=dff),
        grid=(bsz, seq // tm),
        in_specs=[
            pl.BlockSpec((1, tm, d), lambda b, s: (b, s, 0)),
            pl.BlockSpec((1, FFN_HALO, d), lambda b, s: (b, jnp.maximum(s * hb - 1, 0), 0)),
            pl.BlockSpec((1, 1, d), lambda b, s: (layer, 0, 0)),
            pl.BlockSpec((1, d, 2 * dff), lambda b, s: (layer, 0, 0), pipeline_mode=pl.Buffered(1)),
            pl.BlockSpec((1, ffn_w.shape[1], dff), lambda b, s: (layer, 0, 0)),
            pl.BlockSpec((1, 1, dff), lambda b, s: (layer, 0, 0)),
            pl.BlockSpec((1, dff, d), lambda b, s: (layer, 0, 0), pipeline_mode=pl.Buffered(1)),
        ],
        out_specs=pl.BlockSpec((1, tm, d), lambda b, s: (b, s, 0)),
        out_shape=jax.ShapeDtypeStruct((bsz, seq, d), F32),
        scratch_shapes=[
            pltpu.VMEM((tm + FFN_HALO, d), F32),
            pltpu.VMEM((tm + FFN_HALO, d), BF16),
            pltpu.VMEM((tm, d), F32),
            pltpu.VMEM((2, FFN_CHUNK // LANES, tm + FFN_HALO, LANES), F32),
            pltpu.VMEM((2, tm, FFN_CHUNK), F32),
            pltpu.VMEM((2, tm, FFN_CHUNK), BF16),
        ],
        compiler_params=_params("parallel", "parallel"),
        name="ffn",
    )(x, x, g2, w_up, ffn_w, ffn_b, w_down)


def kernel(x, norm1_g, w_in, q_norm_g, k_norm_g, conv_dw_w, conv_dw_b, conv_ln_g, conv_ln_b,
           w_out, norm2_g, w_up, ffn_dw_w, ffn_dw_b, w_down):
    depth = w_in.shape[0]
    aw = w_out.shape[1] - conv_dw_w.shape[2]
    heads = aw // HEAD_DIM
    row3 = lambda p: p[:, None, :]

    w_in_b, w_out_b, w_up_b, w_down_b = (w.astype(BF16) for w in (w_in, w_out, w_up, w_down))
    qg = row3(jnp.tile(q_norm_g, (1, heads)))
    kg = row3(jnp.tile(k_norm_g, (1, heads)))
    g1, g2 = row3(norm1_g), row3(norm2_g)
    cb, lg, lb, fb = row3(conv_dw_b), row3(conv_ln_g), row3(conv_ln_b), row3(ffn_dw_b)

    idx = jnp.arange(2 * LANES)
    bd = (idx[:, None] // HEAD_DIM == idx[None, :] // HEAD_DIM).astype(BF16)
    tix = jnp.arange(ATTN_TILE)
    u = (tix[:, None] > tix[None, :]).astype(BF16)
    causal_bias = jnp.where(tix[None, :] < tix[:, None], 0.0, MASK_BIAS).astype(F32)
    causal_bias = jnp.concatenate([causal_bias, causal_bias], axis=0)
    bias = jnp.stack([jnp.zeros_like(causal_bias), causal_bias])

    for layer in range(depth):
        q, kt, v, c = _proj(x, g1, w_in_b, qg, kg, bd, layer, aw)
        attn = _attn(q, kt, v, u, bias)
        x = _mix(x, attn, c, conv_dw_w, cb, lg, lb, w_out_b, layer)
        x = _ffn(x, g2, w_up_b, ffn_dw_w, fb, w_down_b, layer)
    return x
```

```python
import functools

import jax
import jax.numpy as jnp
from jax import lax
from jax.experimental import pallas as pl
from jax.experimental.pallas import tpu as pltpu

F32 = jnp.float32
BF16 = jnp.bfloat16

EPS = 1e-6
HEAD_DIM = 64
LANES = 128
ATTN_TILE = 256
ROW_TILE = 1024
MIX_ROW_TILE = 512
CONV_HALO = 32
FFN_HALO = 16
FFN_CHUNK = 256
CONV_ROWS = 32
VMEM_LIMIT = 56 * 1024 * 1024
MASK_BIAS = -1e30
LOG2E = 1.4426950408889634


def _params(*sem):
    return pltpu.CompilerParams(dimension_semantics=sem, vmem_limit_bytes=VMEM_LIMIT)


def _dot(a, b):
    return jnp.dot(a, b, preferred_element_type=F32)


def _proj_body(x_ref, g1_ref, w_ref, qg_ref, kg_ref, bd_ref, q_ref, kt_ref, v_ref, c_ref, *, aw):
    x = x_ref[0]
    ms = jnp.mean(x * x, axis=-1, keepdims=True)
    h = (x * lax.rsqrt(ms + EPS) * g1_ref[0]).astype(BF16)

    def head_norm(t, g):
        sq = (t * t).astype(BF16)
        half = 2 * LANES
        ss = jnp.concatenate(
            [_dot(sq[:, i * half:(i + 1) * half], bd_ref[...]) for i in range(aw // half)], axis=-1)
        return t * lax.rsqrt(ss * (1.0 / HEAD_DIM) + EPS) * g

    q = head_norm(_dot(h, w_ref[0, :, 0:aw]), qg_ref[0])
    q = (q * (HEAD_DIM ** -0.5 * LOG2E)).astype(BF16)
    k = head_norm(_dot(h, w_ref[0, :, aw:2 * aw]), kg_ref[0])
    for i in range(k.shape[0] // ATTN_TILE):
        kt_ref[0, i] = k[i * ATTN_TILE:(i + 1) * ATTN_TILE, :].T.astype(BF16)
    v = _dot(h, w_ref[0, :, 2 * aw:3 * aw]).astype(BF16)
    for p in range(aw // LANES):
        q_ref[0, p] = q[:, p * LANES:(p + 1) * LANES]
        v_ref[0, p] = v[:, p * LANES:(p + 1) * LANES]
    cw = (w_ref.shape[2] - 3 * aw) // 2
    a = _dot(h, w_ref[0, :, 3 * aw:3 * aw + cw])
    b = _dot(h, w_ref[0, :, 3 * aw + cw:3 * aw + 2 * cw])
    c_ref[0] = a * jax.nn.sigmoid(b)


def _proj(x, g1, w_in, qg, kg, bd, layer, aw):
    bsz, seq, d = x.shape
    tm = min(ROW_TILE, seq)
    cw = (w_in.shape[2] - 3 * aw) // 2
    nkt = tm // ATTN_TILE
    nhp = aw // LANES
    return pl.pallas_call(
        functools.partial(_proj_body, aw=aw),
        grid=(bsz, seq // tm),
        in_specs=[
            pl.BlockSpec((1, tm, d), lambda b, s: (b, s, 0)),
            pl.BlockSpec((1, 1, d), lambda b, s: (layer, 0, 0)),
            pl.BlockSpec((1, d, w_in.shape[2]), lambda b, s: (layer, 0, 0),
                         pipeline_mode=pl.Buffered(1)),
            pl.BlockSpec((1, 1, aw), lambda b, s: (layer, 0, 0)),
            pl.BlockSpec((1, 1, aw), lambda b, s: (layer, 0, 0)),
            pl.BlockSpec(bd.shape, lambda b, s: (0, 0)),
        ],
        out_specs=[
            pl.BlockSpec((1, nhp, tm, LANES), lambda b, s: (b, 0, s, 0)),
            pl.BlockSpec((1, nkt, aw, ATTN_TILE), lambda b, s: (b, s, 0, 0)),
            pl.BlockSpec((1, nhp, tm, LANES), lambda b, s: (b, 0, s, 0)),
            pl.BlockSpec((1, tm, cw), lambda b, s: (b, s, 0)),
        ],
        out_shape=[
            jax.ShapeDtypeStruct((bsz, nhp, seq, LANES), BF16),
            jax.ShapeDtypeStruct((bsz, seq // ATTN_TILE, aw, ATTN_TILE), BF16),
            jax.ShapeDtypeStruct((bsz, nhp, seq, LANES), BF16),
            jax.ShapeDtypeStruct((bsz, seq, cw), F32),
        ],
        compiler_params=_params("parallel", "parallel"),
        name="proj",
    )(x, g1, w_in, qg, kg, bd)


def _attn_body(q_ref, kt_ref, v_ref, u_ref, bias_ref, o_ref, zs_ref, hi_ref, xb_ref, ts_ref, w_ref,
               acc_ref, carry_ref, *, nq, nhp):
    t = ATTN_TILE
    lane = lax.broadcasted_iota(jnp.int32, (t, LANES), 1)
    low_head = lane < HEAD_DIM
    for ref in (zs_ref, hi_ref, ts_ref, w_ref, acc_ref, carry_ref):
        ref[...] = jnp.zeros_like(ref)
    xb_ref[...] = jnp.full(xb_ref.shape, MASK_BIAS, F32)

    def advance(meta):
        hp, qi, j = meta
        row_done = j == 0
        qn = jnp.where(row_done, qi + 1, qi)
        jn = jnp.where(row_done, qi + 1, j - 1)
        pair_done = qn >= nq
        hn = jnp.where(pair_done, hp + 1, hp)
        qn = jnp.where(pair_done, 0, qn)
        jn = jnp.where(pair_done, 0, jn)
        over = hn >= nhp
        return jnp.where(over, nhp - 1, hn), jnp.where(over, nq - 1, qn), jnp.where(over, 0, jn)

    def substep(p, metas):
        m1, e1, _, _, m3 = metas
        hp, qi, j = m3
        pv = _dot(w_ref[p], v_ref[0, hp, pl.ds(pl.multiple_of(j * t, t), t), :])
        acc = jnp.where(j == qi, pv, acc_ref[...] + pv)
        acc_ref[...] = acc
        o_ref[0, hp, pl.ds(pl.multiple_of(qi * t, t), t), :] = (
            jnp.where(low_head, acc[0:t], acc[t:2 * t]).astype(o_ref.dtype))
        w_ref[1 - p] = jnp.exp2(xb_ref[1 - p] + ts_ref[1 - p]).astype(BF16)
        ts_ref[p] = _dot(hi_ref[p], u_ref[...])
        diag = e1[2] == e1[1]
        z = zs_ref[1 - p] + bias_ref[diag.astype(jnp.int32)]
        neg_abs = lax.bitcast_convert_type(
            lax.bitcast_convert_type(z, jnp.uint32) | jnp.uint32(0x80000000), F32)
        lb = jnp.minimum(z, 0.0) - jnp.log2(1.0 + jnp.exp2(neg_abs))
        l1 = lb - z
        hi_ref[1 - p] = l1.astype(BF16)
        cin = jnp.where(diag, 0.0, carry_ref[...])
        xb_ref[1 - p] = lb + jnp.concatenate([cin] * (t // LANES), axis=-1)
        carry_ref[...] = cin + jnp.sum(l1, axis=-1, keepdims=True)
        hp, qi, j = m1
        q = q_ref[0, hp, pl.ds(pl.multiple_of(qi * t, t), t), :]
        zero = jnp.zeros_like(q)
        qq = jnp.concatenate([jnp.where(low_head, q, zero), jnp.where(low_head, zero, q)], axis=0)
        zs_ref[p] = _dot(qq, kt_ref[0, j, hp])
        return advance(m1), m1, e1, metas[2], metas[3]

    def step(_, metas):
        return substep(1, substep(0, substep(1, substep(0, metas))))

    n_tiles = nhp * (nq * (nq + 1) // 2)
    assert (n_tiles + 4) % 4 == 0
    m0 = (jnp.int32(0), jnp.int32(0), jnp.int32(0))
    lax.fori_loop(0, (n_tiles + 4) // 4, step, (m0,) * 5)


def _attn(q, kt, v, u, bias):
    bsz, nhp, seq, _ = q.shape
    t = ATTN_TILE
    nq = seq // t
    whole = lambda shape: pl.BlockSpec((1,) + shape, lambda b: (b,) + (0,) * len(shape))
    pair = lambda dt: pltpu.VMEM((2, 2 * t, t), dt)
    return pl.pallas_call(
        functools.partial(_attn_body, nq=nq, nhp=nhp),
        grid=(bsz,),
        in_specs=[
            whole((nhp, seq, LANES)),
            whole((nq, nhp, LANES, t)),
            whole((nhp, seq, LANES)),
            pl.BlockSpec((t, t), lambda b: (0, 0)),
            pl.BlockSpec((2, 2 * t, t), lambda b: (0, 0, 0)),
        ],
        out_specs=whole((nhp, seq, LANES)),
        out_shape=jax.ShapeDtypeStruct((bsz, nhp, seq, LANES), BF16),
        scratch_shapes=[pair(F32), pair(BF16), pair(F32), pair(F32), pair(BF16),
                        pltpu.VMEM((2 * t, LANES), F32), pltpu.VMEM((2 * t, LANES), F32)],
        compiler_params=_params("parallel"),
        name="attn",
    )(q, kt.reshape(bsz, nq, nhp, LANES, t), v, u, bias)


def _mix_body(x_ref, a_ref, c_ref, ch_ref, cw_ref, cb_ref, lg_ref, lb_ref, wo_ref, o_ref,
              xp_ref, cs_ref, *, aw):
    tm = x_ref.shape[1]
    kw = cw_ref.shape[1]
    si = pl.program_id(1)
    ngrp = xp_ref.shape[0]
    for g in range(ngrp):
        lanes = slice(g * LANES, (g + 1) * LANES)
        xp_ref[g, 0:CONV_HALO] = jnp.where(si > 0, ch_ref[0, :, lanes], 0.0)
        xp_ref[g, CONV_HALO:] = c_ref[0, :, lanes]
    bias = cb_ref[0]
    lng = lg_ref[0]
    lnb = lb_ref[0]
    off = CONV_HALO - (kw - 1)
    for r0 in range(0, tm, CONV_ROWS):
        parts = []
        for g in range(ngrp):
            lanes = slice(g * LANES, (g + 1) * LANES)
            part = jnp.broadcast_to(bias[:, lanes], (CONV_ROWS, LANES))
            for k in range(kw):
                part = part + cw_ref[0, k:k + 1, lanes] * xp_ref[g, r0 + off + k:r0 + off + k + CONV_ROWS, :]
            parts.append(part)
        acc = jnp.concatenate(parts, axis=-1)
        mu = jnp.mean(acc, axis=-1, keepdims=True)
        xc = acc - mu
        var = jnp.mean(xc * xc, axis=-1, keepdims=True)
        y = xc * lax.rsqrt(var + EPS) * lng + lnb
        cs_ref[r0:r0 + CONV_ROWS] = (y * jax.nn.sigmoid(y)).astype(BF16)
    attn = jnp.concatenate([a_ref[0, p] for p in range(a_ref.shape[1])], axis=-1)
    mixed = _dot(attn, wo_ref[0, 0:aw, :]) + _dot(cs_ref[...], wo_ref[0, aw:, :])
    o_ref[0] = x_ref[0] + mixed


def _mix(x, attn, c, conv_w, conv_b, ln_g, ln_b, w_out, layer):
    bsz, seq, d = x.shape
    nhp = attn.shape[1]
    aw = nhp * LANES
    cw = c.shape[2]
    kw = conv_w.shape[1]
    tm = min(MIX_ROW_TILE, seq)
    hb = tm // CONV_HALO
    vec = lambda n: pl.BlockSpec((1, 1, n), lambda b, s: (layer, 0, 0))
    return pl.pallas_call(
        functools.partial(_mix_body, aw=aw),
        grid=(bsz, seq // tm),
        in_specs=[
            pl.BlockSpec((1, tm, d), lambda b, s: (b, s, 0)),
            pl.BlockSpec((1, nhp, tm, LANES), lambda b, s: (b, 0, s, 0)),
            pl.BlockSpec((1, tm, cw), lambda b, s: (b, s, 0)),
            pl.BlockSpec((1, CONV_HALO, cw), lambda b, s: (b, jnp.maximum(s * hb - 1, 0), 0)),
            pl.BlockSpec((1, kw, cw), lambda b, s: (layer, 0, 0)),
            vec(cw), vec(cw), vec(cw),
            pl.BlockSpec((1, d, d), lambda b, s: (layer, 0, 0), pipeline_mode=pl.Buffered(1)),
        ],
        out_specs=pl.BlockSpec((1, tm, d), lambda b, s: (b, s, 0)),
        out_shape=jax.ShapeDtypeStruct((bsz, seq, d), F32),
        scratch_shapes=[pltpu.VMEM((cw // LANES, tm + CONV_HALO, LANES), F32),
                        pltpu.VMEM((tm, cw), BF16)],
        compiler_params=_params("parallel", "parallel"),
        name="mix",
    )(x, attn, c, c, conv_w, conv_b, ln_g, ln_b, w_out)


def _ffn_body(x_ref, xh_ref, g2_ref, wu_ref, fw_ref, fb_ref, wd_ref, o_ref, xe_ref, h_ref, acc_ref,
              graw_ref, vraw_ref, act_ref, *, dff):
    tm = x_ref.shape[1]
    n = FFN_CHUNK
    nchunks = dff // n
    si = pl.program_id(1)
    xe_ref[0:FFN_HALO] = jnp.where(si > 0, xh_ref[0], 0.0)
    xe_ref[FFN_HALO:] = x_ref[0]
    xe = xe_ref[...]
    ms = jnp.mean(xe * xe, axis=-1, keepdims=True)
    h_ref[...] = (xe * lax.rsqrt(ms + EPS) * g2_ref[0]).astype(BF16)
    acc_ref[...] = jnp.zeros_like(acc_ref)

    def col(c, base=0):
        return base + c * n if isinstance(c, int) else pl.multiple_of(base + c * n, n)

    def step(c, p, up=True, gate=True, down=True):
        if gate:
            fw = fw_ref[0, :, pl.ds(col(c - 1), n)]
            fb = fb_ref[0, :, pl.ds(col(c - 1), n)]
            taps = fw.shape[0]
            for lg in range(n // LANES):
                lanes = slice(lg * LANES, (lg + 1) * LANES)
                conv = fb[:, lanes]
                for k in range(taps):
                    r0 = FFN_HALO - (taps - 1 - k)
                    conv = conv + fw[k:k + 1, lanes] * graw_ref[1 - p, lg, r0:r0 + tm, :]
                act_ref[1 - p, :, lanes] = (
                    conv * jax.nn.sigmoid(conv) * vraw_ref[1 - p, :, lanes]).astype(BF16)
        if down:
            acc_ref[...] += _dot(act_ref[p], wd_ref[0, pl.ds(col(c - 2), n), :])
        if up:
            g = _dot(h_ref[...], wu_ref[0, :, pl.ds(col(c), n)])
            for lg in range(n // LANES):
                graw_ref[p, lg] = g[:, lg * LANES:(lg + 1) * LANES]
            vraw_ref[p] = _dot(h_ref[FFN_HALO:], wu_ref[0, :, pl.ds(col(c, dff), n)])

    step(0, 0, gate=False, down=False)
    step(1, 1, down=False)
    quads = (nchunks - 2) // 4

    def body(i, _):
        for k in range(4):
            step(2 + 4 * i + k, k % 2)
        return 0

    lax.fori_loop(0, quads, body, 0)
    for c in range(2 + 4 * quads, nchunks + 2):
        step(c, c % 2, up=c < nchunks, gate=c - 1 < nchunks)
    o_ref[0] = x_ref[0] + acc_ref[...]


def _ffn(x, g2, w_up, ffn_w, ffn_b, w_down, layer):
    bsz, seq, d = x.shape
    dff = w_down.shape[1]
    tm = min(ROW_TILE, seq)
    hb = tm // FFN_HALO
    return pl.pallas_call(
        functools.partial(_ffn_body, dff=dff),
        grid=(bsz, seq // tm),
        in_specs=[
            pl.BlockSpec((1, tm, d), lambda b, s: (b, s, 0)),
            pl.BlockSpec((1, FFN_HALO, d), lambda b, s: (b, jnp.maximum(s * hb - 1, 0), 0)),
            pl.BlockSpec((1, 1, d), lambda b, s: (layer, 0, 0)),
            pl.BlockSpec((1, d, 2 * dff), lambda b, s: (layer, 0, 0), pipeline_mode=pl.Buffered(1)),
            pl.BlockSpec((1, ffn_w.shape[1], dff), lambda b, s: (layer, 0, 0)),
            pl.BlockSpec((1, 1, dff), lambda b, s: (layer, 0, 0)),
            pl.BlockSpec((1, dff, d), lambda b, s: (layer, 0, 0), pipeline_mode=pl.Buffered(1)),
        ],
        out_specs=pl.BlockSpec((1, tm, d), lambda b, s: (b, s, 0)),
        out_shape=jax.ShapeDtypeStruct((bsz, seq, d), F32),
        scratch_shapes=[
            pltpu.VMEM((tm + FFN_HALO, d), F32),
            pltpu.VMEM((tm + FFN_HALO, d), BF16),
            pltpu.VMEM((tm, d), F32),
            pltpu.VMEM((2, FFN_CHUNK // LANES, tm + FFN_HALO, LANES), F32),
            pltpu.VMEM((2, tm, FFN_CHUNK), F32),
            pltpu.VMEM((2, tm, FFN_CHUNK), BF16),
        ],
        compiler_params=_params("parallel", "parallel"),
        name="ffn",
    )(x, x, g2, w_up, ffn_w, ffn_b, w_down)


def kernel(x, norm1_g, w_in, q_norm_g, k_norm_g, conv_dw_w, conv_dw_b, conv_ln_g, conv_ln_b,
           w_out, norm2_g, w_up, ffn_dw_w, ffn_dw_b, w_down):
    depth = w_in.shape[0]
    aw = w_out.shape[1] - conv_dw_w.shape[2]
    heads = aw // HEAD_DIM
    row3 = lambda p: p[:, None, :]

    w_in_b, w_out_b, w_up_b, w_down_b = (w.astype(BF16) for w in (w_in, w_out, w_up, w_down))
    qg = row3(jnp.tile(q_norm_g, (1, heads)))
    kg = row3(jnp.tile(k_norm_g, (1, heads)))
    g1, g2 = row3(norm1_g), row3(norm2_g)
    cb, lg, lb, fb = row3(conv_dw_b), row3(conv_ln_g), row3(conv_ln_b), row3(ffn_dw_b)

    idx = jnp.arange(2 * LANES)
    bd = (idx[:, None] // HEAD_DIM == idx[None, :] // HEAD_DIM).astype(BF16)
    tix = jnp.arange(ATTN_TILE)
    u = (tix[:, None] > tix[None, :]).astype(BF16)
    causal_bias = jnp.where(tix[None, :] < tix[:, None], 0.0, MASK_BIAS).astype(F32)
    causal_bias = jnp.concatenate([causal_bias, causal_bias], axis=0)
    bias = jnp.stack([jnp.zeros_like(causal_bias), causal_bias])

    for layer in range(depth):
        q, kt, v, c = _proj(x, g1, w_in_b, qg, kg, bd, layer, aw)
        attn = _attn(q, kt, v, u, bias)
        x = _mix(x, attn, c, conv_dw_w, cb, lg, lb, w_out_b, layer)
        x = _ffn(x, g2, w_up_b, ffn_dw_w, fb, w_down_b, layer)
    return x
```

```python
import functools

import jax
import jax.numpy as jnp
from jax import lax
from jax.experimental import pallas as pl
from jax.experimental.pallas import tpu as pltpu

F32 = jnp.float32
BF16 = jnp.bfloat16

EPS = 1e-6
HEAD_DIM = 64
LANES = 128
ATTN_TILE = 256
ROW_TILE = 1024
MIX_ROW_TILE = 512
CONV_HALO = 32
FFN_HALO = 16
FFN_CHUNK = 256
CONV_ROWS = 32
VMEM_LIMIT = 56 * 1024 * 1024
F32_MAX = 3.4028234663852886e38
LOG2E = 1.4426950408889634


def _params(*sem):
    return pltpu.CompilerParams(dimension_semantics=sem, vmem_limit_bytes=VMEM_LIMIT)


def _dot(a, b):
    return jnp.dot(a, b, preferred_element_type=F32)


def _proj_body(x_ref, g1_ref, w_ref, qg_ref, kg_ref, bd_ref, q_ref, kt_ref, v_ref, c_ref, *, aw):
    x = x_ref[0]
    ms = jnp.mean(x * x, axis=-1, keepdims=True)
    h = (x * lax.rsqrt(ms + EPS) * g1_ref[0]).astype(BF16)

    def head_norm(t, g):
        sq = (t * t).astype(BF16)
        half = 2 * LANES
        ss = jnp.concatenate(
            [_dot(sq[:, i * half:(i + 1) * half], bd_ref[...]) for i in range(aw // half)], axis=-1)
        return t * lax.rsqrt(ss * (1.0 / HEAD_DIM) + EPS) * g

    cw = (w_ref.shape[2] - 3 * aw) // 2
    a = _dot(h, w_ref[0, :, 3 * aw:3 * aw + cw])
    b = _dot(h, w_ref[0, :, 3 * aw + cw:3 * aw + 2 * cw])
    c_ref[0] = a * jax.nn.sigmoid(b)
    q = head_norm(_dot(h, w_ref[0, :, 0:aw]), qg_ref[0])
    q = (q * (HEAD_DIM ** -0.5 * LOG2E)).astype(BF16)
    k = head_norm(_dot(h, w_ref[0, :, aw:2 * aw]), kg_ref[0])
    for i in range(k.shape[0] // ATTN_TILE):
        kt_ref[0, i] = k[i * ATTN_TILE:(i + 1) * ATTN_TILE, :].T.astype(BF16)
    v = _dot(h, w_ref[0, :, 2 * aw:3 * aw]).astype(BF16)
    for p in range(aw // LANES):
        q_ref[0, p] = q[:, p * LANES:(p + 1) * LANES]
        v_ref[0, p] = v[:, p * LANES:(p + 1) * LANES]


def _proj(x, g1, w_in, qg, kg, bd, layer, aw):
    bsz, seq, d = x.shape
    tm = min(ROW_TILE, seq)
    cw = (w_in.shape[2] - 3 * aw) // 2
    nkt = tm // ATTN_TILE
    nhp = aw // LANES
    return pl.pallas_call(
        functools.partial(_proj_body, aw=aw),
        grid=(bsz, seq // tm),
        in_specs=[
            pl.BlockSpec((1, tm, d), lambda b, s: (b, s, 0)),
            pl.BlockSpec((1, 1, d), lambda b, s: (layer, 0, 0)),
            pl.BlockSpec((1, d, w_in.shape[2]), lambda b, s: (layer, 0, 0),
                         pipeline_mode=pl.Buffered(1)),
            pl.BlockSpec((1, 1, aw), lambda b, s: (layer, 0, 0)),
            pl.BlockSpec((1, 1, aw), lambda b, s: (layer, 0, 0)),
            pl.BlockSpec(bd.shape, lambda b, s: (0, 0)),
        ],
        out_specs=[
            pl.BlockSpec((1, nhp, tm, LANES), lambda b, s: (b, 0, s, 0)),
            pl.BlockSpec((1, nkt, aw, ATTN_TILE), lambda b, s: (b, s, 0, 0)),
            pl.BlockSpec((1, nhp, tm, LANES), lambda b, s: (b, 0, s, 0)),
            pl.BlockSpec((1, tm, cw), lambda b, s: (b, s, 0)),
        ],
        out_shape=[
            jax.ShapeDtypeStruct((bsz, nhp, seq, LANES), BF16),
            jax.ShapeDtypeStruct((bsz, seq // ATTN_TILE, aw, ATTN_TILE), BF16),
            jax.ShapeDtypeStruct((bsz, nhp, seq, LANES), BF16),
            jax.ShapeDtypeStruct((bsz, seq, cw), F32),
        ],
        compiler_params=_params("parallel", "parallel"),
        name="proj",
    )(x, g1, w_in, qg, kg, bd)


def _attn_body(q_ref, kt_ref, v_ref, u_ref, cap_ref, o_ref, zs_ref, hi_ref, xb_ref, ts_ref, w_ref,
               acc_ref, carry_ref, *, nq, nhp):
    t = ATTN_TILE
    lane = lax.broadcasted_iota(jnp.int32, (t, LANES), 1)
    low_head = lane < HEAD_DIM
    for ref in (zs_ref, hi_ref, ts_ref, w_ref, acc_ref, carry_ref):
        ref[...] = jnp.zeros_like(ref)
    xb_ref[...] = jnp.full(xb_ref.shape, -F32_MAX, F32)

    def advance(meta):
        hp, qi, j = meta
        row_done = j == 0
        qn = jnp.where(row_done, qi + 1, qi)
        jn = jnp.where(row_done, qi + 1, j - 1)
        pair_done = qn >= nq
        hn = jnp.where(pair_done, hp + 1, hp)
        qn = jnp.where(pair_done, 0, qn)
        jn = jnp.where(pair_done, 0, jn)
        over = hn >= nhp
        return jnp.where(over, nhp - 1, hn), jnp.where(over, nq - 1, qn), jnp.where(over, 0, jn)

    def substep(p, metas):
        m1, e1, _, _, m3 = metas
        hp, qi, j = m3
        pv = _dot(w_ref[p], v_ref[0, hp, pl.ds(pl.multiple_of(j * t, t), t), :])
        acc = jnp.where(j == qi, pv, acc_ref[...] + pv)
        acc_ref[...] = acc
        o_ref[0, hp, pl.ds(pl.multiple_of(qi * t, t), t), :] = (
            jnp.where(low_head, acc[0:t], acc[t:2 * t]).astype(o_ref.dtype))
        w_ref[1 - p] = jnp.exp2(xb_ref[1 - p] + ts_ref[1 - p]).astype(BF16)
        ts_ref[p] = _dot(hi_ref[p], u_ref[...])
        diag = e1[2] == e1[1]
        z = jnp.minimum(zs_ref[1 - p], cap_ref[diag.astype(jnp.int32)])
        neg_abs = lax.bitcast_convert_type(
            lax.bitcast_convert_type(z, jnp.uint32) | jnp.uint32(0x80000000), F32)
        lb = jnp.minimum(z, 0.0) - jnp.log2(1.0 + jnp.exp2(neg_abs))
        l1 = lb - z
        hi_ref[1 - p] = l1.astype(BF16)
        cin = jnp.where(diag, 0.0, carry_ref[...])
        xb_ref[1 - p] = lb + jnp.concatenate([cin] * (t // LANES), axis=-1)
        carry_ref[...] = cin + jnp.sum(l1, axis=-1, keepdims=True)
        hp, qi, j = m1
        q = q_ref[0, hp, pl.ds(pl.multiple_of(qi * t, t), t), :]
        zero = jnp.zeros_like(q)
        qq = jnp.concatenate([jnp.where(low_head, q, zero), jnp.where(low_head, zero, q)], axis=0)
        zs_ref[p] = _dot(qq, kt_ref[0, j, hp])
        return advance(m1), m1, e1, metas[2], metas[3]

    def step(_, metas):
        return substep(1, substep(0, substep(1, substep(0, metas))))

    n_tiles = nhp * (nq * (nq + 1) // 2)
    assert (n_tiles + 4) % 4 == 0
    m0 = (jnp.int32(0), jnp.int32(0), jnp.int32(0))
    lax.fori_loop(0, (n_tiles + 4) // 4, step, (m0,) * 5)


def _attn(q, kt, v, u, cap):
    bsz, nhp, seq, _ = q.shape
    t = ATTN_TILE
    nq = seq // t
    whole = lambda shape: pl.BlockSpec((1,) + shape, lambda b: (b,) + (0,) * len(shape))
    pair = lambda dt: pltpu.VMEM((2, 2 * t, t), dt)
    return pl.pallas_call(
        functools.partial(_attn_body, nq=nq, nhp=nhp),
        grid=(bsz,),
        in_specs=[
            whole((nhp, seq, LANES)),
            whole((nq, nhp, LANES, t)),
            whole((nhp, seq, LANES)),
            pl.BlockSpec((t, t), lambda b: (0, 0)),
            pl.BlockSpec((2, 2 * t, t), lambda b: (0, 0, 0)),
        ],
        out_specs=whole((nhp, seq, LANES)),
        out_shape=jax.ShapeDtypeStruct((bsz, nhp, seq, LANES), BF16),
        scratch_shapes=[pair(F32), pair(BF16), pair(F32), pair(F32), pair(BF16),
                        pltpu.VMEM((2 * t, LANES), F32), pltpu.VMEM((2 * t, LANES), F32)],
        compiler_params=_params("parallel"),
        name="attn",
    )(q, kt.reshape(bsz, nq, nhp, LANES, t), v, u, cap)


def _mix_body(x_ref, a_ref, c_ref, ch_ref, cw_ref, cb_ref, lg_ref, lb_ref, wo_ref, o_ref,
              xp_ref, cs_ref, *, aw):
    tm = x_ref.shape[1]
    kw = cw_ref.shape[1]
    si = pl.program_id(1)
    ngrp = xp_ref.shape[0]
    for g in range(ngrp):
        lanes = slice(g * LANES, (g + 1) * LANES)
        xp_ref[g, 0:CONV_HALO] = jnp.where(si > 0, ch_ref[0, :, lanes], 0.0)
        xp_ref[g, CONV_HALO:] = c_ref[0, :, lanes]
    bias = cb_ref[0]
    lng = lg_ref[0]
    lnb = lb_ref[0]
    off = CONV_HALO - (kw - 1)
    for r0 in range(0, tm, CONV_ROWS):
        parts = []
        for g in range(ngrp):
            lanes = slice(g * LANES, (g + 1) * LANES)
            part = jnp.broadcast_to(bias[:, lanes], (CONV_ROWS, LANES))
            for k in range(kw):
                part = part + cw_ref[0, k:k + 1, lanes] * xp_ref[g, r0 + off + k:r0 + off + k + CONV_ROWS, :]
            parts.append(part)
        acc = jnp.concatenate(parts, axis=-1)
        mu = jnp.mean(acc, axis=-1, keepdims=True)
        xc = acc - mu
        var = jnp.mean(xc * xc, axis=-1, keepdims=True)
        y = xc * lax.rsqrt(var + EPS) * lng + lnb
        cs_ref[r0:r0 + CONV_ROWS] = (y * jax.nn.sigmoid(y)).astype(BF16)
    attn = jnp.concatenate([a_ref[0, p] for p in range(a_ref.shape[1])], axis=-1)
    mixed = _dot(attn, wo_ref[0, 0:aw, :]) + _dot(cs_ref[...], wo_ref[0, aw:, :])
    o_ref[0] = x_ref[0] + mixed


def _mix(x, attn, c, conv_w, conv_b, ln_g, ln_b, w_out, layer):
    bsz, seq, d = x.shape
    nhp = attn.shape[1]
    aw = nhp * LANES
    cw = c.shape[2]
    kw = conv_w.shape[1]
    tm = min(MIX_ROW_TILE, seq)
    hb = tm // CONV_HALO
    vec = lambda n: pl.BlockSpec((1, 1, n), lambda b, s: (layer, 0, 0))
    return pl.pallas_call(
        functools.partial(_mix_body, aw=aw),
        grid=(bsz, seq // tm),
        in_specs=[
            pl.BlockSpec((1, tm, d), lambda b, s: (b, s, 0)),
            pl.BlockSpec((1, nhp, tm, LANES), lambda b, s: (b, 0, s, 0)),
            pl.BlockSpec((1, tm, cw), lambda b, s: (b, s, 0)),
            pl.BlockSpec((1, CONV_HALO, cw), lambda b, s: (b, jnp.maximum(s * hb - 1, 0), 0)),
            pl.BlockSpec((1, kw, cw), lambda b, s: (layer, 0, 0)),
            vec(cw), vec(cw), vec(cw),
            pl.BlockSpec((1, d, d), lambda b, s: (layer, 0, 0), pipeline_mode=pl.Buffered(1)),
        ],
        out_specs=pl.BlockSpec((1, tm, d), lambda b, s: (b, s, 0)),
        out_shape=jax.ShapeDtypeStruct((bsz, seq, d), F32),
        scratch_shapes=[pltpu.VMEM((cw // LANES, tm + CONV_HALO, LANES), F32),
                        pltpu.VMEM((tm, cw), BF16)],
        compiler_params=_params("parallel", "parallel"),
        name="mix",
    )(x, attn, c, c, conv_w, conv_b, ln_g, ln_b, w_out)


def _ffn_body(x_ref, xh_ref, g2_ref, wu_ref, fw_ref, fb_ref, wd_ref, o_ref, xe_ref, h_ref, acc_ref,
              graw_ref, vraw_ref, act_ref, *, dff):
    tm = x_ref.shape[1]
    n = FFN_CHUNK
    nchunks = dff // n
    si = pl.program_id(1)
    xe_ref[0:FFN_HALO] = jnp.where(si > 0, xh_ref[0], 0.0)
    xe_ref[FFN_HALO:] = x_ref[0]
    xe = xe_ref[...]
    ms = jnp.mean(xe * xe, axis=-1, keepdims=True)
    h_ref[...] = (xe * lax.rsqrt(ms + EPS) * g2_ref[0]).astype(BF16)
    acc_ref[...] = jnp.zeros_like(acc_ref)

    def col(c, base=0):
        return base + c * n if isinstance(c, int) else pl.multiple_of(base + c * n, n)

    def step(c, p, up=True, gate=True, down=True):
        if gate:
            fw = fw_ref[0, :, pl.ds(col(c - 1), n)]
            fb = fb_ref[0, :, pl.ds(col(c - 1), n)]
            taps = fw.shape[0]
            for lg in range(n // LANES):
                lanes = slice(lg * LANES, (lg + 1) * LANES)
                conv = fb[:, lanes]
                for k in range(taps):
                    r0 = FFN_HALO - (taps - 1 - k)
                    conv = conv + fw[k:k + 1, lanes] * graw_ref[1 - p, lg, r0:r0 + tm, :]
                act_ref[1 - p, :, lanes] = (
                    conv * jax.nn.sigmoid(conv) * vraw_ref[1 - p, :, lanes]).astype(BF16)
        if down:
            acc_ref[...] += _dot(act_ref[p], wd_ref[0, pl.ds(col(c - 2), n), :])
        if up:
            g = _dot(h_ref[...], wu_ref[0, :, pl.ds(col(c), n)])
            for lg in range(n // LANES):
                graw_ref[p, lg] = g[:, lg * LANES:(lg + 1) * LANES]
            vraw_ref[p] = _dot(h_ref[FFN_HALO:], wu_ref[0, :, pl.ds(col(c, dff), n)])

    step(0, 0, gate=False, down=False)
    step(1, 1, down=False)
    quads = (nchunks - 2) // 4

    def body(i, _):
        for k in range(4):
            step(2 + 4 * i + k, k % 2)
        return 0

    lax.fori_loop(0, quads, body, 0)
    for c in range(2 + 4 * quads, nchunks + 2):
        step(c, c % 2, up=c < nchunks, gate=c - 1 < nchunks)
    o_ref[0] = x_ref[0] + acc_ref[...]


def _ffn(x, g2, w_up, ffn_w, ffn_b, w_down, layer):
    bsz, seq, d = x.shape
    dff = w_down.shape[1]
    tm = min(ROW_TILE, seq)
    hb = tm // FFN_HALO
    return pl.pallas_call(
        functools.partial(_ffn_body, dff=dff),
        grid=(bsz, seq // tm),
        in_specs=[
            pl.BlockSpec((1, tm, d), lambda b, s: (b, s, 0)),
            pl.BlockSpec((1, FFN_HALO, d), lambda b, s: (b, jnp.maximum(s * hb - 1, 0), 0)),
            pl.BlockSpec((1, 1, d), lambda b, s: (layer, 0, 0)),
            pl.BlockSpec((1, d, 2 * dff), lambda b, s: (layer, 0, 0), pipeline_mode=pl.Buffered(1)),
            pl.BlockSpec((1, ffn_w.shape[1], dff), lambda b, s: (layer, 0, 0)),
            pl.BlockSpec((1, 1, dff), lambda b, s: (layer, 0, 0)),
            pl.BlockSpec((1, dff, d), lambda b, s: (layer, 0, 0), pipeline_mode=pl.Buffered(1)),
        ],
        out_specs=pl.BlockSpec((1, tm, d), lambda b, s: (b, s, 0)),
        out_shape=jax.ShapeDtypeStruct((bsz, seq, d), F32),
        scratch_shapes=[
            pltpu.VMEM((tm + FFN_HALO, d), F32),
            pltpu.VMEM((tm + FFN_HALO, d), BF16),
            pltpu.VMEM((tm, d), F32),
            pltpu.VMEM((2, FFN_CHUNK // LANES, tm + FFN_HALO, LANES), F32),
            pltpu.VMEM((2, tm, FFN_CHUNK), F32),
            pltpu.VMEM((2, tm, FFN_CHUNK), BF16),
        ],
        compiler_params=_params("parallel", "parallel"),
        name="ffn",
    )(x, x, g2, w_up, ffn_w, ffn_b, w_down)


def kernel(x, norm1_g, w_in, q_norm_g, k_norm_g, conv_dw_w, conv_dw_b, conv_ln_g, conv_ln_b,
           w_out, norm2_g, w_up, ffn_dw_w, ffn_dw_b, w_down):
    depth = w_in.shape[0]
    aw = w_out.shape[1] - conv_dw_w.shape[2]
    heads = aw // HEAD_DIM
    row3 = lambda p: p[:, None, :]

    w_in_b, w_out_b, w_up_b, w_down_b = (w.astype(BF16) for w in (w_in, w_out, w_up, w_down))
    qg = row3(jnp.tile(q_norm_g, (1, heads)))
    kg = row3(jnp.tile(k_norm_g, (1, heads)))
    g1, g2 = row3(norm1_g), row3(norm2_g)
    cb, lg, lb, fb = row3(conv_dw_b), row3(conv_ln_g), row3(conv_ln_b), row3(ffn_dw_b)

    idx = jnp.arange(2 * LANES)
    bd = (idx[:, None] // HEAD_DIM == idx[None, :] // HEAD_DIM).astype(BF16)
    tix = jnp.arange(ATTN_TILE)
    u = (tix[:, None] > tix[None, :]).astype(BF16)
    causal_cap = jnp.where(tix[None, :] < tix[:, None], F32_MAX, -F32_MAX).astype(F32)
    causal_cap = jnp.concatenate([causal_cap, causal_cap], axis=0)
    cap = jnp.stack([jnp.full_like(causal_cap, F32_MAX), causal_cap])

    for layer in range(depth):
        q, kt, v, c = _proj(x, g1, w_in_b, qg, kg, bd, layer, aw)
        attn = _attn(q, kt, v, u, cap)
        x = _mix(x, attn, c, conv_dw_w, cb, lg, lb, w_out_b, layer)
        x = _ffn(x, g2, w_up_b, ffn_dw_w, fb, w_down_b, layer)
    return x
```

```python
import functools

import jax
import jax.numpy as jnp
from jax import lax
from jax.experimental import pallas as pl
from jax.experimental.pallas import tpu as pltpu

F32 = jnp.float32
BF16 = jnp.bfloat16

EPS = 1e-6
HEAD_DIM = 64
LANES = 128
ATTN_TILE = 256
ATTN_UNROLL = 16
ROW_TILE = 1024
MIX_ROW_TILE = 512
CONV_HALO = 32
FFN_HALO = 16
FFN_CHUNK = 256
CONV_ROWS = 32
VMEM_LIMIT = 56 * 1024 * 1024
F32_MAX = 3.4028234663852886e38
LOG2E = 1.4426950408889634


def _params(*sem):
    return pltpu.CompilerParams(dimension_semantics=sem, vmem_limit_bytes=VMEM_LIMIT)


def _dot(a, b):
    return jnp.dot(a, b, preferred_element_type=F32)


def _proj_body(x_ref, g1_ref, w_ref, qg_ref, kg_ref, bd_ref, q_ref, kt_ref, v_ref, c_ref, *, aw):
    x = x_ref[0]
    ms = jnp.mean(x * x, axis=-1, keepdims=True)
    h = (x * lax.rsqrt(ms + EPS) * g1_ref[0]).astype(BF16)

    def head_norm(t, g):
        sq = (t * t).astype(BF16)
        half = 2 * LANES
        ss = jnp.concatenate(
            [_dot(sq[:, i * half:(i + 1) * half], bd_ref[...]) for i in range(aw // half)], axis=-1)
        return t * lax.rsqrt(ss * (1.0 / HEAD_DIM) + EPS) * g

    cw = (w_ref.shape[2] - 3 * aw) // 2
    a = _dot(h, w_ref[0, :, 3 * aw:3 * aw + cw])
    b = _dot(h, w_ref[0, :, 3 * aw + cw:3 * aw + 2 * cw])
    c_ref[0] = a * jax.nn.sigmoid(b)
    q = head_norm(_dot(h, w_ref[0, :, 0:aw]), qg_ref[0])
    q = (q * (HEAD_DIM ** -0.5 * LOG2E)).astype(BF16)
    k = head_norm(_dot(h, w_ref[0, :, aw:2 * aw]), kg_ref[0])
    for i in range(k.shape[0] // ATTN_TILE):
        kt_ref[0, i] = k[i * ATTN_TILE:(i + 1) * ATTN_TILE, :].T.astype(BF16)
    v = _dot(h, w_ref[0, :, 2 * aw:3 * aw]).astype(BF16)
    for p in range(aw // LANES):
        q_ref[0, p] = q[:, p * LANES:(p + 1) * LANES]
        v_ref[0, p] = v[:, p * LANES:(p + 1) * LANES]


def _proj(x, g1, w_in, qg, kg, bd, layer, aw):
    bsz, seq, d = x.shape
    tm = min(ROW_TILE, seq)
    cw = (w_in.shape[2] - 3 * aw) // 2
    nkt = tm // ATTN_TILE
    nhp = aw // LANES
    return pl.pallas_call(
        functools.partial(_proj_body, aw=aw),
        grid=(bsz, seq // tm),
        in_specs=[
            pl.BlockSpec((1, tm, d), lambda b, s: (b, s, 0)),
            pl.BlockSpec((1, 1, d), lambda b, s: (layer, 0, 0)),
            pl.BlockSpec((1, d, w_in.shape[2]), lambda b, s: (layer, 0, 0),
                         pipeline_mode=pl.Buffered(1)),
            pl.BlockSpec((1, 1, aw), lambda b, s: (layer, 0, 0)),
            pl.BlockSpec((1, 1, aw), lambda b, s: (layer, 0, 0)),
            pl.BlockSpec(bd.shape, lambda b, s: (0, 0)),
        ],
        out_specs=[
            pl.BlockSpec((1, nhp, tm, LANES), lambda b, s: (b, 0, s, 0)),
            pl.BlockSpec((1, nkt, aw, ATTN_TILE), lambda b, s: (b, s, 0, 0)),
            pl.BlockSpec((1, nhp, tm, LANES), lambda b, s: (b, 0, s, 0)),
            pl.BlockSpec((1, tm, cw), lambda b, s: (b, s, 0)),
        ],
        out_shape=[
            jax.ShapeDtypeStruct((bsz, nhp, seq, LANES), BF16),
            jax.ShapeDtypeStruct((bsz, seq // ATTN_TILE, aw, ATTN_TILE), BF16),
            jax.ShapeDtypeStruct((bsz, nhp, seq, LANES), BF16),
            jax.ShapeDtypeStruct((bsz, seq, cw), F32),
        ],
        compiler_params=_params("parallel", "parallel"),
        name="proj",
    )(x, g1, w_in, qg, kg, bd)


def _attn_body(q_ref, kt_ref, v_ref, u_ref, cap_ref, o_ref, zs_ref, hi_ref, xb_ref, ts_ref, w_ref,
               acc_ref, carry_ref, *, nq, nhp):
    t = ATTN_TILE
    lane = lax.broadcasted_iota(jnp.int32, (t, LANES), 1)
    low_head = lane < HEAD_DIM
    for ref in (zs_ref, hi_ref, ts_ref, w_ref, acc_ref, carry_ref):
        ref[...] = jnp.zeros_like(ref)
    xb_ref[...] = jnp.full(xb_ref.shape, -F32_MAX, F32)

    def advance(meta):
        hp, qi, j = meta
        row_done = j == 0
        qn = jnp.where(row_done, qi + 1, qi)
        jn = jnp.where(row_done, qi + 1, j - 1)
        pair_done = qn >= nq
        hn = jnp.where(pair_done, hp + 1, hp)
        qn = jnp.where(pair_done, 0, qn)
        jn = jnp.where(pair_done, 0, jn)
        over = hn >= nhp
        return jnp.where(over, nhp - 1, hn), jnp.where(over, nq - 1, qn), jnp.where(over, 0, jn)

    def substep(p, metas):
        m1, e1, _, _, m3 = metas
        hp, qi, j = m3
        pv = _dot(w_ref[p], v_ref[0, hp, pl.ds(pl.multiple_of(j * t, t), t), :])
        acc = jnp.where(j == qi, pv, acc_ref[...] + pv)
        acc_ref[...] = acc
        o_ref[0, hp, pl.ds(pl.multiple_of(qi * t, t), t), :] = (
            jnp.where(low_head, acc[0:t], acc[t:2 * t]).astype(o_ref.dtype))
        w_ref[1 - p] = jnp.exp2(xb_ref[1 - p] + ts_ref[1 - p]).astype(BF16)
        ts_ref[p] = _dot(hi_ref[p], u_ref[...])
        diag = e1[2] == e1[1]
        z = jnp.minimum(zs_ref[1 - p], cap_ref[diag.astype(jnp.int32)])
        neg_abs = lax.bitcast_convert_type(
            lax.bitcast_convert_type(z, jnp.uint32) | jnp.uint32(0x80000000), F32)
        lb = jnp.minimum(z, 0.0) - jnp.log2(1.0 + jnp.exp2(neg_abs))
        l1 = lb - z
        hi_ref[1 - p] = l1.astype(BF16)
        cin = jnp.where(diag, 0.0, carry_ref[...])
        xb_ref[1 - p] = lb + jnp.concatenate([cin] * (t // LANES), axis=-1)
        carry_ref[...] = cin + jnp.sum(l1, axis=-1, keepdims=True)
        hp, qi, j = m1
        q = q_ref[0, hp, pl.ds(pl.multiple_of(qi * t, t), t), :]
        zero = jnp.zeros_like(q)
        qq = jnp.concatenate([jnp.where(low_head, q, zero), jnp.where(low_head, zero, q)], axis=0)
        zs_ref[p] = _dot(qq, kt_ref[0, j, hp])
        return advance(m1), m1, e1, metas[2], metas[3]

    def steps(count, metas):
        for s in range(count):
            metas = substep(s % 2, metas)
        return metas

    n_steps = nhp * (nq * (nq + 1) // 2) + 4
    assert n_steps % 2 == 0
    m0 = (jnp.int32(0), jnp.int32(0), jnp.int32(0))
    metas = lax.fori_loop(0, n_steps // ATTN_UNROLL, lambda _, m: steps(ATTN_UNROLL, m), (m0,) * 5)
    steps(n_steps % ATTN_UNROLL, metas)


def _attn(q, kt, v, u, cap):
    bsz, nhp, seq, _ = q.shape
    t = ATTN_TILE
    nq = seq // t
    whole = lambda shape: pl.BlockSpec((1,) + shape, lambda b: (b,) + (0,) * len(shape))
    pair = lambda dt: pltpu.VMEM((2, 2 * t, t), dt)
    return pl.pallas_call(
        functools.partial(_attn_body, nq=nq, nhp=nhp),
        grid=(bsz,),
        in_specs=[
            whole((nhp, seq, LANES)),
            whole((nq, nhp, LANES, t)),
            whole((nhp, seq, LANES)),
            pl.BlockSpec((t, t), lambda b: (0, 0)),
            pl.BlockSpec((2, 2 * t, t), lambda b: (0, 0, 0)),
        ],
        out_specs=whole((nhp, seq, LANES)),
        out_shape=jax.ShapeDtypeStruct((bsz, nhp, seq, LANES), BF16),
        scratch_shapes=[pair(F32), pair(BF16), pair(F32), pair(F32), pair(BF16),
                        pltpu.VMEM((2 * t, LANES), F32), pltpu.VMEM((2 * t, LANES), F32)],
        compiler_params=_params("parallel"),
        name="attn",
    )(q, kt.reshape(bsz, nq, nhp, LANES, t), v, u, cap)


def _mix_body(x_ref, a_ref, c_ref, ch_ref, cw_ref, cb_ref, lg_ref, lb_ref, wo_ref, o_ref,
              xp_ref, cs_ref, *, aw):
    tm = x_ref.shape[1]
    kw = cw_ref.shape[1]
    si = pl.program_id(1)
    ngrp = xp_ref.shape[0]
    for g in range(ngrp):
        lanes = slice(g * LANES, (g + 1) * LANES)
        xp_ref[g, 0:CONV_HALO] = jnp.where(si > 0, ch_ref[0, :, lanes], 0.0)
        xp_ref[g, CONV_HALO:] = c_ref[0, :, lanes]
    bias = cb_ref[0]
    lng = lg_ref[0]
    lnb = lb_ref[0]
    off = CONV_HALO - (kw - 1)
    for r0 in range(0, tm, CONV_ROWS):
        parts = []
        for g in range(ngrp):
            lanes = slice(g * LANES, (g + 1) * LANES)
            part = jnp.broadcast_to(bias[:, lanes], (CONV_ROWS, LANES))
            for k in range(kw):
                part = part + cw_ref[0, k:k + 1, lanes] * xp_ref[g, r0 + off + k:r0 + off + k + CONV_ROWS, :]
            parts.append(part)
        acc = jnp.concatenate(parts, axis=-1)
        mu = jnp.mean(acc, axis=-1, keepdims=True)
        xc = acc - mu
        var = jnp.mean(xc * xc, axis=-1, keepdims=True)
        y = xc * lax.rsqrt(var + EPS) * lng + lnb
        cs_ref[r0:r0 + CONV_ROWS] = (y * jax.nn.sigmoid(y)).astype(BF16)
    attn = jnp.concatenate([a_ref[0, p] for p in range(a_ref.shape[1])], axis=-1)
    mixed = _dot(attn, wo_ref[0, 0:aw, :]) + _dot(cs_ref[...], wo_ref[0, aw:, :])
    o_ref[0] = x_ref[0] + mixed


def _mix(x, attn, c, conv_w, conv_b, ln_g, ln_b, w_out, layer):
    bsz, seq, d = x.shape
    nhp = attn.shape[1]
    aw = nhp * LANES
    cw = c.shape[2]
    kw = conv_w.shape[1]
    tm = min(MIX_ROW_TILE, seq)
    hb = tm // CONV_HALO
    vec = lambda n: pl.BlockSpec((1, 1, n), lambda b, s: (layer, 0, 0))
    return pl.pallas_call(
        functools.partial(_mix_body, aw=aw),
        grid=(bsz, seq // tm),
        in_specs=[
            pl.BlockSpec((1, tm, d), lambda b, s: (b, s, 0)),
            pl.BlockSpec((1, nhp, tm, LANES), lambda b, s: (b, 0, s, 0)),
            pl.BlockSpec((1, tm, cw), lambda b, s: (b, s, 0)),
            pl.BlockSpec((1, CONV_HALO, cw), lambda b, s: (b, jnp.maximum(s * hb - 1, 0), 0)),
            pl.BlockSpec((1, kw, cw), lambda b, s: (layer, 0, 0)),
            vec(cw), vec(cw), vec(cw),
            pl.BlockSpec((1, d, d), lambda b, s: (layer, 0, 0), pipeline_mode=pl.Buffered(1)),
        ],
        out_specs=pl.BlockSpec((1, tm, d), lambda b, s: (b, s, 0)),
        out_shape=jax.ShapeDtypeStruct((bsz, seq, d), F32),
        scratch_shapes=[pltpu.VMEM((cw // LANES, tm + CONV_HALO, LANES), F32),
                        pltpu.VMEM((tm, cw), BF16)],
        compiler_params=_params("parallel", "parallel"),
        name="mix",
    )(x, attn, c, c, conv_w, conv_b, ln_g, ln_b, w_out)


def _ffn_body(x_ref, xh_ref, g2_ref, wu_ref, fw_ref, fb_ref, wd_ref, o_ref, xe_ref, h_ref, acc_ref,
              graw_ref, vraw_ref, act_ref, *, dff):
    tm = x_ref.shape[1]
    n = FFN_CHUNK
    nchunks = dff // n
    si = pl.program_id(1)
    xe_ref[0:FFN_HALO] = jnp.where(si > 0, xh_ref[0], 0.0)
    xe_ref[FFN_HALO:] = x_ref[0]
    xe = xe_ref[...]
    ms = jnp.mean(xe * xe, axis=-1, keepdims=True)
    h_ref[...] = (xe * lax.rsqrt(ms + EPS) * g2_ref[0]).astype(BF16)
    acc_ref[...] = jnp.zeros_like(acc_ref)

    def col(c, base=0):
        return base + c * n if isinstance(c, int) else pl.multiple_of(base + c * n, n)

    def step(c, p, up=True, gate=True, down=True):
        if gate:
            fw = fw_ref[0, :, pl.ds(col(c - 1), n)]
            fb = fb_ref[0, :, pl.ds(col(c - 1), n)]
            taps = fw.shape[0]
            for lg in range(n // LANES):
                lanes = slice(lg * LANES, (lg + 1) * LANES)
                conv = fb[:, lanes]
                for k in range(taps):
                    r0 = FFN_HALO - (taps - 1 - k)
                    conv = conv + fw[k:k + 1, lanes] * graw_ref[1 - p, lg, r0:r0 + tm, :]
                act_ref[1 - p, :, lanes] = (
                    conv * jax.nn.sigmoid(conv) * vraw_ref[1 - p, :, lanes]).astype(BF16)
        if down:
            acc_ref[...] += _dot(act_ref[p], wd_ref[0, pl.ds(col(c - 2), n), :])
        if up:
            g = _dot(h_ref[...], wu_ref[0, :, pl.ds(col(c), n)])
            for lg in range(n // LANES):
                graw_ref[p, lg] = g[:, lg * LANES:(lg + 1) * LANES]
            vraw_ref[p] = _dot(h_ref[FFN_HALO:], wu_ref[0, :, pl.ds(col(c, dff), n)])

    step(0, 0, gate=False, down=False)
    step(1, 1, down=False)
    quads = (nchunks - 2) // 4

    def body(i, _):
        for k in range(4):
            step(2 + 4 * i + k, k % 2)
        return 0

    lax.fori_loop(0, quads, body, 0)
    for c in range(2 + 4 * quads, nchunks + 2):
        step(c, c % 2, up=c < nchunks, gate=c - 1 < nchunks)
    o_ref[0] = x_ref[0] + acc_ref[...]


def _ffn(x, g2, w_up, ffn_w, ffn_b, w_down, layer):
    bsz, seq, d = x.shape
    dff = w_down.shape[1]
    tm = min(ROW_TILE, seq)
    hb = tm // FFN_HALO
    return pl.pallas_call(
        functools.partial(_ffn_body, dff=dff),
        grid=(bsz, seq // tm),
        in_specs=[
            pl.BlockSpec((1, tm, d), lambda b, s: (b, s, 0)),
            pl.BlockSpec((1, FFN_HALO, d), lambda b, s: (b, jnp.maximum(s * hb - 1, 0), 0)),
            pl.BlockSpec((1, 1, d), lambda b, s: (layer, 0, 0)),
            pl.BlockSpec((1, d, 2 * dff), lambda b, s: (layer, 0, 0), pipeline_mode=pl.Buffered(1)),
            pl.BlockSpec((1, ffn_w.shape[1], dff), lambda b, s: (layer, 0, 0)),
            pl.BlockSpec((1, 1, dff), lambda b, s: (layer, 0, 0)),
            pl.BlockSpec((1, dff, d), lambda b, s: (layer, 0, 0), pipeline_mode=pl.Buffered(1)),
        ],
        out_specs=pl.BlockSpec((1, tm, d), lambda b, s: (b, s, 0)),
        out_shape=jax.ShapeDtypeStruct((bsz, seq, d), F32),
        scratch_shapes=[
            pltpu.VMEM((tm + FFN_HALO, d), F32),
            pltpu.VMEM((tm + FFN_HALO, d), BF16),
            pltpu.VMEM((tm, d), F32),
            pltpu.VMEM((2, FFN_CHUNK // LANES, tm + FFN_HALO, LANES), F32),
            pltpu.VMEM((2, tm, FFN_CHUNK), F32),
            pltpu.VMEM((2, tm, FFN_CHUNK), BF16),
        ],
        compiler_params=_params("parallel", "parallel"),
        name="ffn",
    )(x, x, g2, w_up, ffn_w, ffn_b, w_down)


def kernel(x, norm1_g, w_in, q_norm_g, k_norm_g, conv_dw_w, conv_dw_b, conv_ln_g, conv_ln_b,
           w_out, norm2_g, w_up, ffn_dw_w, ffn_dw_b, w_down):
    depth = w_in.shape[0]
    aw = w_out.shape[1] - conv_dw_w.shape[2]
    heads = aw // HEAD_DIM
    row3 = lambda p: p[:, None, :]

    w_in_b, w_out_b, w_up_b, w_down_b = (w.astype(BF16) for w in (w_in, w_out, w_up, w_down))
    qg = row3(jnp.tile(q_norm_g, (1, heads)))
    kg = row3(jnp.tile(k_norm_g, (1, heads)))
    g1, g2 = row3(norm1_g), row3(norm2_g)
    cb, lg, lb, fb = row3(conv_dw_b), row3(conv_ln_g), row3(conv_ln_b), row3(ffn_dw_b)

    idx = jnp.arange(2 * LANES)
    bd = (idx[:, None] // HEAD_DIM == idx[None, :] // HEAD_DIM).astype(BF16)
    tix = jnp.arange(ATTN_TILE)
    u = (tix[:, None] > tix[None, :]).astype(BF16)
    causal_cap = jnp.where(tix[None, :] < tix[:, None], F32_MAX, -F32_MAX).astype(F32)
    causal_cap = jnp.concatenate([causal_cap, causal_cap], axis=0)
    cap = jnp.stack([jnp.full_like(causal_cap, F32_MAX), causal_cap])

    for layer in range(depth):
        q, kt, v, c = _proj(x, g1, w_in_b, qg, kg, bd, layer, aw)
        attn = _attn(q, kt, v, u, cap)
        x = _mix(x, attn, c, conv_dw_w, cb, lg, lb, w_out_b, layer)
        x = _ffn(x, g2, w_up_b, ffn_dw_w, fb, w_down_b, layer)
    return x
```

```python
import functools

import jax
import jax.numpy as jnp
from jax import lax
from jax.experimental import pallas as pl
from jax.experimental.pallas import tpu as pltpu

F32 = jnp.float32
BF16 = jnp.bfloat16

EPS = 1e-6
HEAD_DIM = 64
LANES = 128
ATTN_TILE = 256
ATTN_UNROLL = 16
ROW_TILE = 1024
MIX_ROW_TILE = 512
CONV_HALO = 32
FFN_HALO = 16
FFN_CHUNK = 256
CONV_ROWS = 32
VMEM_LIMIT = 56 * 1024 * 1024
F32_MAX = 3.4028234663852886e38
LOG2E = 1.4426950408889634


def _params(*sem):
    return pltpu.CompilerParams(dimension_semantics=sem, vmem_limit_bytes=VMEM_LIMIT)


def _dot(a, b):
    return jnp.dot(a, b, preferred_element_type=F32)


def _proj_body(x_ref, g1_ref, w_ref, qg_ref, kg_ref, bd_ref, q_ref, kt_ref, v_ref, c_ref, *, aw):
    x = x_ref[0]
    ms = jnp.mean(x * x, axis=-1, keepdims=True)
    h = (x * lax.rsqrt(ms + EPS) * g1_ref[0]).astype(BF16)

    def head_norm(t, g):
        sq = (t * t).astype(BF16)
        half = 2 * LANES
        ss = jnp.concatenate(
            [_dot(sq[:, i * half:(i + 1) * half], bd_ref[...]) for i in range(aw // half)], axis=-1)
        return t * lax.rsqrt(ss * (1.0 / HEAD_DIM) + EPS) * g

    cw = (w_ref.shape[2] - 3 * aw) // 2
    a = _dot(h, w_ref[0, :, 3 * aw:3 * aw + cw])
    b = _dot(h, w_ref[0, :, 3 * aw + cw:3 * aw + 2 * cw])
    c_ref[0] = a * jax.nn.sigmoid(b)
    q = head_norm(_dot(h, w_ref[0, :, 0:aw]), qg_ref[0])
    q = (q * (HEAD_DIM ** -0.5 * LOG2E)).astype(BF16)
    k = head_norm(_dot(h, w_ref[0, :, aw:2 * aw]), kg_ref[0])
    for i in range(k.shape[0] // ATTN_TILE):
        kt_ref[0, i] = k[i * ATTN_TILE:(i + 1) * ATTN_TILE, :].T.astype(BF16)
    v = _dot(h, w_ref[0, :, 2 * aw:3 * aw]).astype(BF16)
    for p in range(aw // LANES):
        q_ref[0, p] = q[:, p * LANES:(p + 1) * LANES]
        v_ref[0, p] = v[:, p * LANES:(p + 1) * LANES]


def _proj(x, g1, w_in, qg, kg, bd, layer, aw):
    bsz, seq, d = x.shape
    tm = min(ROW_TILE, seq)
    cw = (w_in.shape[2] - 3 * aw) // 2
    nkt = tm // ATTN_TILE
    nhp = aw // LANES
    return pl.pallas_call(
        functools.partial(_proj_body, aw=aw),
        grid=(bsz, seq // tm),
        in_specs=[
            pl.BlockSpec((1, tm, d), lambda b, s: (b, s, 0)),
            pl.BlockSpec((1, 1, d), lambda b, s: (layer, 0, 0)),
            pl.BlockSpec((1, d, w_in.shape[2]), lambda b, s: (layer, 0, 0),
                         pipeline_mode=pl.Buffered(1)),
            pl.BlockSpec((1, 1, aw), lambda b, s: (layer, 0, 0)),
            pl.BlockSpec((1, 1, aw), lambda b, s: (layer, 0, 0)),
            pl.BlockSpec(bd.shape, lambda b, s: (0, 0)),
        ],
        out_specs=[
            pl.BlockSpec((1, nhp, tm, LANES), lambda b, s: (b, 0, s, 0)),
            pl.BlockSpec((1, nkt, aw, ATTN_TILE), lambda b, s: (b, s, 0, 0)),
            pl.BlockSpec((1, nhp, tm, LANES), lambda b, s: (b, 0, s, 0)),
            pl.BlockSpec((1, tm, cw), lambda b, s: (b, s, 0)),
        ],
        out_shape=[
            jax.ShapeDtypeStruct((bsz, nhp, seq, LANES), BF16),
            jax.ShapeDtypeStruct((bsz, seq // ATTN_TILE, aw, ATTN_TILE), BF16),
            jax.ShapeDtypeStruct((bsz, nhp, seq, LANES), BF16),
            jax.ShapeDtypeStruct((bsz, seq, cw), F32),
        ],
        compiler_params=_params("parallel", "parallel"),
        name="proj",
    )(x, g1, w_in, qg, kg, bd)


def _attn_body(q_ref, kt_ref, v_ref, u_ref, cap_ref, o_ref, zs_ref, hi_ref, xb_ref, ts_ref, w_ref,
               acc_ref, carry_ref, *, nq, nhp):
    t = ATTN_TILE
    lane = lax.broadcasted_iota(jnp.int32, (t, LANES), 1)
    low_head = lane < HEAD_DIM
    for ref in (zs_ref, hi_ref, ts_ref, w_ref, acc_ref, carry_ref):
        ref[...] = jnp.zeros_like(ref)
    xb_ref[...] = jnp.full(xb_ref.shape, -F32_MAX, F32)

    def advance(meta):
        hp, qi, j = meta
        row_done = j == 0
        qn = jnp.where(row_done, qi + 1, qi)
        jn = jnp.where(row_done, qi + 1, j - 1)
        pair_done = qn >= nq
        hn = jnp.where(pair_done, hp + 1, hp)
        qn = jnp.where(pair_done, 0, qn)
        jn = jnp.where(pair_done, 0, jn)
        over = hn >= nhp
        return jnp.where(over, nhp - 1, hn), jnp.where(over, nq - 1, qn), jnp.where(over, 0, jn)

    def substep(p, metas):
        m1, e1, _, _, m3 = metas
        hp, qi, j = m3
        pv = _dot(w_ref[p], v_ref[0, hp, pl.ds(pl.multiple_of(j * t, t), t), :])
        acc = jnp.where(j == qi, pv, acc_ref[...] + pv)
        acc_ref[...] = acc
        o_ref[0, hp, pl.ds(pl.multiple_of(qi * t, t), t), :] = (
            jnp.where(low_head, acc[0:t], acc[t:2 * t]).astype(o_ref.dtype))
        w_ref[1 - p] = jnp.exp2(xb_ref[1 - p] + ts_ref[1 - p]).astype(BF16)
        ts_ref[p] = _dot(hi_ref[p], u_ref[...])
        diag = e1[2] == e1[1]
        z = jnp.minimum(zs_ref[1 - p], cap_ref[diag.astype(jnp.int32)])
        neg_abs = lax.bitcast_convert_type(
            lax.bitcast_convert_type(z, jnp.uint32) | jnp.uint32(0x80000000), F32)
        lb = jnp.minimum(z, 0.0) - jnp.log2(1.0 + jnp.exp2(neg_abs))
        l1 = lb - z
        hi_ref[1 - p] = l1.astype(BF16)
        cin = jnp.where(diag, 0.0, carry_ref[...])
        xb_ref[1 - p] = lb + jnp.concatenate([cin] * (t // LANES), axis=-1)
        carry_ref[...] = cin + jnp.sum(l1, axis=-1, keepdims=True)
        hp, qi, j = m1
        q = q_ref[0, hp, pl.ds(pl.multiple_of(qi * t, t), t), :]
        zero = jnp.zeros_like(q)
        qq = jnp.concatenate([jnp.where(low_head, q, zero), jnp.where(low_head, zero, q)], axis=0)
        zs_ref[p] = _dot(qq, kt_ref[0, j, hp])
        return advance(m1), m1, e1, metas[2], metas[3]

    def steps(count, metas):
        for s in range(count):
            metas = substep(s % 2, metas)
        return metas

    n_steps = nhp * (nq * (nq + 1) // 2) + 4
    assert n_steps % 2 == 0
    m0 = (jnp.int32(0), jnp.int32(0), jnp.int32(0))
    metas = lax.fori_loop(0, n_steps // ATTN_UNROLL, lambda _, m: steps(ATTN_UNROLL, m), (m0,) * 5)
    steps(n_steps % ATTN_UNROLL, metas)


def _attn(q, kt, v, u, cap):
    bsz, nhp, seq, _ = q.shape
    t = ATTN_TILE
    nq = seq // t
    whole = lambda shape: pl.BlockSpec((1,) + shape, lambda b: (b,) + (0,) * len(shape))
    pair = lambda dt: pltpu.VMEM((2, 2 * t, t), dt)
    return pl.pallas_call(
        functools.partial(_attn_body, nq=nq, nhp=nhp),
        grid=(bsz,),
        in_specs=[
            whole((nhp, seq, LANES)),
            whole((nq, nhp, LANES, t)),
            whole((nhp, seq, LANES)),
            pl.BlockSpec((t, t), lambda b: (0, 0)),
            pl.BlockSpec((2, 2 * t, t), lambda b: (0, 0, 0)),
        ],
        out_specs=whole((nhp, seq, LANES)),
        out_shape=jax.ShapeDtypeStruct((bsz, nhp, seq, LANES), BF16),
        scratch_shapes=[pair(F32), pair(BF16), pair(F32), pair(F32), pair(BF16),
                        pltpu.VMEM((2 * t, LANES), F32), pltpu.VMEM((2 * t, LANES), F32)],
        compiler_params=_params("parallel"),
        name="attn",
    )(q, kt.reshape(bsz, nq, nhp, LANES, t), v, u, cap)


def _mix_body(x_ref, a_ref, c_ref, ch_ref, cw_ref, cb_ref, lg_ref, lb_ref, wo_ref, o_ref,
              xp_ref, cs_ref, *, aw):
    tm = x_ref.shape[1]
    kw = cw_ref.shape[1]
    si = pl.program_id(1)
    ngrp = xp_ref.shape[0]
    for g in range(ngrp):
        lanes = slice(g * LANES, (g + 1) * LANES)
        xp_ref[g, 0:CONV_HALO] = jnp.where(si > 0, ch_ref[0, :, lanes], 0.0)
        xp_ref[g, CONV_HALO:] = c_ref[0, :, lanes]
    bias = cb_ref[0]
    lng = lg_ref[0]
    lnb = lb_ref[0]
    off = CONV_HALO - (kw - 1)
    for r0 in range(0, tm, CONV_ROWS):
        parts = []
        for g in range(ngrp):
            lanes = slice(g * LANES, (g + 1) * LANES)
            part = jnp.broadcast_to(bias[:, lanes], (CONV_ROWS, LANES))
            for k in range(kw):
                part = part + cw_ref[0, k:k + 1, lanes] * xp_ref[g, r0 + off + k:r0 + off + k + CONV_ROWS, :]
            parts.append(part)
        acc = jnp.concatenate(parts, axis=-1)
        mu = jnp.mean(acc, axis=-1, keepdims=True)
        xc = acc - mu
        var = jnp.mean(xc * xc, axis=-1, keepdims=True)
        y = xc * lax.rsqrt(var + EPS) * lng + lnb
        cs_ref[r0:r0 + CONV_ROWS] = (y * jax.nn.sigmoid(y)).astype(BF16)
    attn = jnp.concatenate([a_ref[0, p] for p in range(a_ref.shape[1])], axis=-1)
    mixed = _dot(attn, wo_ref[0, 0:aw, :]) + _dot(cs_ref[...], wo_ref[0, aw:, :])
    o_ref[0] = x_ref[0] + mixed


def _mix(x, attn, c, conv_w, conv_b, ln_g, ln_b, w_out, layer):
    bsz, seq, d = x.shape
    nhp = attn.shape[1]
    aw = nhp * LANES
    cw = c.shape[2]
    kw = conv_w.shape[1]
    tm = min(MIX_ROW_TILE, seq)
    hb = tm // CONV_HALO
    vec = lambda n: pl.BlockSpec((1, 1, n), lambda b, s: (layer, 0, 0))
    return pl.pallas_call(
        functools.partial(_mix_body, aw=aw),
        grid=(bsz, seq // tm),
        in_specs=[
            pl.BlockSpec((1, tm, d), lambda b, s: (b, s, 0)),
            pl.BlockSpec((1, nhp, tm, LANES), lambda b, s: (b, 0, s, 0)),
            pl.BlockSpec((1, tm, cw), lambda b, s: (b, s, 0)),
            pl.BlockSpec((1, CONV_HALO, cw), lambda b, s: (b, jnp.maximum(s * hb - 1, 0), 0)),
            pl.BlockSpec((1, kw, cw), lambda b, s: (layer, 0, 0)),
            vec(cw), vec(cw), vec(cw),
            pl.BlockSpec((1, d, d), lambda b, s: (layer, 0, 0), pipeline_mode=pl.Buffered(1)),
        ],
        out_specs=pl.BlockSpec((1, tm, d), lambda b, s: (b, s, 0)),
        out_shape=jax.ShapeDtypeStruct((bsz, seq, d), F32),
        scratch_shapes=[pltpu.VMEM((cw // LANES, tm + CONV_HALO, LANES), F32),
                        pltpu.VMEM((tm, cw), BF16)],
        compiler_params=_params("parallel", "parallel"),
        name="mix",
    )(x, attn, c, c, conv_w, conv_b, ln_g, ln_b, w_out)


def _ffn_body(x_ref, xh_ref, g2_ref, wu_ref, fw_ref, fb_ref, wd_ref, o_ref, xe_ref, h_ref, acc_ref,
              graw_ref, vraw_ref, act_ref, *, dff):
    tm = x_ref.shape[1]
    n = FFN_CHUNK
    nchunks = dff // n
    si = pl.program_id(1)
    xe_ref[0:FFN_HALO] = jnp.where(si > 0, xh_ref[0], 0.0)
    xe_ref[FFN_HALO:] = x_ref[0]
    xe = xe_ref[...]
    ms = jnp.mean(xe * xe, axis=-1, keepdims=True)
    h_ref[...] = (xe * lax.rsqrt(ms + EPS) * g2_ref[0]).astype(BF16)
    acc_ref[...] = jnp.zeros_like(acc_ref)

    def col(c, base=0):
        return base + c * n

    def step(c, p, up, gate, down):
        if gate:
            fw = fw_ref[0, :, pl.ds(col(c - 1), n)]
            fb = fb_ref[0, :, pl.ds(col(c - 1), n)]
            taps = fw.shape[0]
            for lg in range(n // LANES):
                lanes = slice(lg * LANES, (lg + 1) * LANES)
                conv = fb[:, lanes]
                for k in range(taps):
                    r0 = FFN_HALO - (taps - 1 - k)
                    conv = conv + fw[k:k + 1, lanes] * graw_ref[1 - p, lg, r0:r0 + tm, :]
                act_ref[1 - p, :, lanes] = (
                    conv * jax.nn.sigmoid(conv) * vraw_ref[1 - p, :, lanes]).astype(BF16)
        if down:
            acc_ref[...] += _dot(act_ref[p], wd_ref[0, pl.ds(col(c - 2), n), :])
        if up:
            g = _dot(h_ref[...], wu_ref[0, :, pl.ds(col(c), n)])
            for lg in range(n // LANES):
                graw_ref[p, lg] = g[:, lg * LANES:(lg + 1) * LANES]
            vraw_ref[p] = _dot(h_ref[FFN_HALO:], wu_ref[0, :, pl.ds(col(c, dff), n)])

    for c in range(nchunks + 2):
        step(c, c % 2, up=c < nchunks, gate=1 <= c <= nchunks, down=c >= 2)
    o_ref[0] = x_ref[0] + acc_ref[...]


def _ffn(x, g2, w_up, ffn_w, ffn_b, w_down, layer):
    bsz, seq, d = x.shape
    dff = w_down.shape[1]
    tm = min(ROW_TILE, seq)
    hb = tm // FFN_HALO
    return pl.pallas_call(
        functools.partial(_ffn_body, dff=dff),
        grid=(bsz, seq // tm),
        in_specs=[
            pl.BlockSpec((1, tm, d), lambda b, s: (b, s, 0)),
            pl.BlockSpec((1, FFN_HALO, d), lambda b, s: (b, jnp.maximum(s * hb - 1, 0), 0)),
            pl.BlockSpec((1, 1, d), lambda b, s: (layer, 0, 0)),
            pl.BlockSpec((1, d, 2 * dff), lambda b, s: (layer, 0, 0), pipeline_mode=pl.Buffered(1)),
            pl.BlockSpec((1, ffn_w.shape[1], dff), lambda b, s: (layer, 0, 0)),
            pl.BlockSpec((1, 1, dff), lambda b, s: (layer, 0, 0)),
            pl.BlockSpec((1, dff, d), lambda b, s: (layer, 0, 0), pipeline_mode=pl.Buffered(1)),
        ],
        out_specs=pl.BlockSpec((1, tm, d), lambda b, s: (b, s, 0)),
        out_shape=jax.ShapeDtypeStruct((bsz, seq, d), F32),
        scratch_shapes=[
            pltpu.VMEM((tm + FFN_HALO, d), F32),
            pltpu.VMEM((tm + FFN_HALO, d), BF16),
            pltpu.VMEM((tm, d), F32),
            pltpu.VMEM((2, FFN_CHUNK // LANES, tm + FFN_HALO, LANES), F32),
            pltpu.VMEM((2, tm, FFN_CHUNK), F32),
            pltpu.VMEM((2, tm, FFN_CHUNK), BF16),
        ],
        compiler_params=_params("parallel", "parallel"),
        name="ffn",
    )(x, x, g2, w_up, ffn_w, ffn_b, w_down)


def kernel(x, norm1_g, w_in, q_norm_g, k_norm_g, conv_dw_w, conv_dw_b, conv_ln_g, conv_ln_b,
           w_out, norm2_g, w_up, ffn_dw_w, ffn_dw_b, w_down):
    depth = w_in.shape[0]
    aw = w_out.shape[1] - conv_dw_w.shape[2]
    heads = aw // HEAD_DIM
    row3 = lambda p: p[:, None, :]

    w_in_b, w_out_b, w_up_b, w_down_b = (w.astype(BF16) for w in (w_in, w_out, w_up, w_down))
    qg = row3(jnp.tile(q_norm_g, (1, heads)))
    kg = row3(jnp.tile(k_norm_g, (1, heads)))
    g1, g2 = row3(norm1_g), row3(norm2_g)
    cb, lg, lb, fb = row3(conv_dw_b), row3(conv_ln_g), row3(conv_ln_b), row3(ffn_dw_b)

    idx = jnp.arange(2 * LANES)
    bd = (idx[:, None] // HEAD_DIM == idx[None, :] // HEAD_DIM).astype(BF16)
    tix = jnp.arange(ATTN_TILE)
    u = (tix[:, None] > tix[None, :]).astype(BF16)
    causal_cap = jnp.where(tix[None, :] < tix[:, None], F32_MAX, -F32_MAX).astype(F32)
    causal_cap = jnp.concatenate([causal_cap, causal_cap], axis=0)
    cap = jnp.stack([jnp.full_like(causal_cap, F32_MAX), causal_cap])

    for layer in range(depth):
        q, kt, v, c = _proj(x, g1, w_in_b, qg, kg, bd, layer, aw)
        attn = _attn(q, kt, v, u, cap)
        x = _mix(x, attn, c, conv_dw_w, cb, lg, lb, w_out_b, layer)
        x = _ffn(x, g2, w_up_b, ffn_dw_w, fb, w_down_b, layer)
    return x
```

```python
import functools

import jax
import jax.numpy as jnp
from jax import lax
from jax.experimental import pallas as pl
from jax.experimental.pallas import tpu as pltpu

F32 = jnp.float32
BF16 = jnp.bfloat16

EPS = 1e-6
HEAD_DIM = 64
LANES = 128
ATTN_TILE = 256
ATTN_UNROLL = 36
ROW_TILE = 1024
MIX_ROW_TILE = 512
CONV_HALO = 32
FFN_HALO = 16
FFN_CHUNK = 256
CONV_ROWS = 32
VMEM_LIMIT = 56 * 1024 * 1024
F32_MAX = 3.4028234663852886e38
LOG2E = 1.4426950408889634


def _params(*sem):
    return pltpu.CompilerParams(dimension_semantics=sem, vmem_limit_bytes=VMEM_LIMIT)


def _dot(a, b):
    return jnp.dot(a, b, preferred_element_type=F32)


def _proj_body(x_ref, g1_ref, w_ref, qg_ref, kg_ref, bd_ref, q_ref, kt_ref, v_ref, c_ref, *, aw):
    x = x_ref[0]
    ms = jnp.mean(x * x, axis=-1, keepdims=True)
    h = (x * lax.rsqrt(ms + EPS) * g1_ref[0]).astype(BF16)

    def head_norm(t, g):
        sq = (t * t).astype(BF16)
        half = 2 * LANES
        ss = jnp.concatenate(
            [_dot(sq[:, i * half:(i + 1) * half], bd_ref[...]) for i in range(aw // half)], axis=-1)
        return t * lax.rsqrt(ss * (1.0 / HEAD_DIM) + EPS) * g

    cw = (w_ref.shape[2] - 3 * aw) // 2
    a = _dot(h, w_ref[0, :, 3 * aw:3 * aw + cw])
    b = _dot(h, w_ref[0, :, 3 * aw + cw:3 * aw + 2 * cw])
    c_ref[0] = a * jax.nn.sigmoid(b)
    q = head_norm(_dot(h, w_ref[0, :, 0:aw]), qg_ref[0])
    q = (q * (HEAD_DIM ** -0.5 * LOG2E)).astype(BF16)
    k = head_norm(_dot(h, w_ref[0, :, aw:2 * aw]), kg_ref[0])
    for i in range(k.shape[0] // ATTN_TILE):
        kt_ref[0, i] = k[i * ATTN_TILE:(i + 1) * ATTN_TILE, :].T.astype(BF16)
    v = _dot(h, w_ref[0, :, 2 * aw:3 * aw]).astype(BF16)
    for p in range(aw // LANES):
        q_ref[0, p] = q[:, p * LANES:(p + 1) * LANES]
        v_ref[0, p] = v[:, p * LANES:(p + 1) * LANES]


def _proj(x, g1, w_in, qg, kg, bd, layer, aw):
    bsz, seq, d = x.shape
    tm = min(ROW_TILE, seq)
    cw = (w_in.shape[2] - 3 * aw) // 2
    nkt = tm // ATTN_TILE
    nhp = aw // LANES
    return pl.pallas_call(
        functools.partial(_proj_body, aw=aw),
        grid=(bsz, seq // tm),
        in_specs=[
            pl.BlockSpec((1, tm, d), lambda b, s: (b, s, 0)),
            pl.BlockSpec((1, 1, d), lambda b, s: (layer, 0, 0)),
            pl.BlockSpec((1, d, w_in.shape[2]), lambda b, s: (layer, 0, 0),
                         pipeline_mode=pl.Buffered(1)),
            pl.BlockSpec((1, 1, aw), lambda b, s: (layer, 0, 0)),
            pl.BlockSpec((1, 1, aw), lambda b, s: (layer, 0, 0)),
            pl.BlockSpec(bd.shape, lambda b, s: (0, 0)),
        ],
        out_specs=[
            pl.BlockSpec((1, nhp, tm, LANES), lambda b, s: (b, 0, s, 0)),
            pl.BlockSpec((1, nkt, aw, ATTN_TILE), lambda b, s: (b, s, 0, 0)),
            pl.BlockSpec((1, nhp, tm, LANES), lambda b, s: (b, 0, s, 0)),
            pl.BlockSpec((1, tm, cw), lambda b, s: (b, s, 0)),
        ],
        out_shape=[
            jax.ShapeDtypeStruct((bsz, nhp, seq, LANES), BF16),
            jax.ShapeDtypeStruct((bsz, seq // ATTN_TILE, aw, ATTN_TILE), BF16),
            jax.ShapeDtypeStruct((bsz, nhp, seq, LANES), BF16),
            jax.ShapeDtypeStruct((bsz, seq, cw), F32),
        ],
        compiler_params=_params("parallel", "parallel"),
        name="proj",
    )(x, g1, w_in, qg, kg, bd)


def _attn_body(q_ref, kt_ref, v_ref, u_ref, cap_ref, o_ref, zs_ref, hi_ref, xb_ref, ts_ref, w_ref,
               acc_ref, carry_ref, *, nq, nhp):
    t = ATTN_TILE
    lane = lax.broadcasted_iota(jnp.int32, (t, LANES), 1)
    low_head = lane < HEAD_DIM
    for ref in (zs_ref, hi_ref, ts_ref, w_ref, acc_ref, carry_ref):
        ref[...] = jnp.zeros_like(ref)
    xb_ref[...] = jnp.full(xb_ref.shape, -F32_MAX, F32)

    def advance(meta):
        hp, qi, j = meta
        row_done = j == 0
        qn = jnp.where(row_done, qi + 1, qi)
        jn = jnp.where(row_done, qi + 1, j - 1)
        pair_done = qn >= nq
        hn = jnp.where(pair_done, hp + 1, hp)
        qn = jnp.where(pair_done, 0, qn)
        jn = jnp.where(pair_done, 0, jn)
        over = hn >= nhp
        return jnp.where(over, nhp - 1, hn), jnp.where(over, nq - 1, qn), jnp.where(over, 0, jn)

    def substep(p, metas):
        m1, e1, _, _, m3 = metas
        hp, qi, j = m3
        pv = _dot(w_ref[p], v_ref[0, hp, pl.ds(pl.multiple_of(j * t, t), t), :])
        acc = jnp.where(j == qi, pv, acc_ref[...] + pv)
        acc_ref[...] = acc
        o_ref[0, hp, pl.ds(pl.multiple_of(qi * t, t), t), :] = (
            jnp.where(low_head, acc[0:t], acc[t:2 * t]).astype(o_ref.dtype))
        w_ref[1 - p] = jnp.exp2(xb_ref[1 - p] + ts_ref[1 - p]).astype(BF16)
        ts_ref[p] = _dot(hi_ref[p], u_ref[...])
        diag = e1[2] == e1[1]
        z = jnp.minimum(zs_ref[1 - p], cap_ref[diag.astype(jnp.int32)])
        neg_abs = lax.bitcast_convert_type(
            lax.bitcast_convert_type(z, jnp.uint32) | jnp.uint32(0x80000000), F32)
        lb = jnp.minimum(z, 0.0) - jnp.log2(1.0 + jnp.exp2(neg_abs))
        l1 = lb - z
        hi_ref[1 - p] = l1.astype(BF16)
        cin = jnp.where(diag, 0.0, carry_ref[...])
        xb_ref[1 - p] = lb + jnp.concatenate([cin] * (t // LANES), axis=-1)
        carry_ref[...] = cin + jnp.sum(l1, axis=-1, keepdims=True)
        hp, qi, j = m1
        q = q_ref[0, hp, pl.ds(pl.multiple_of(qi * t, t), t), :]
        zero = jnp.zeros_like(q)
        qq = jnp.concatenate([jnp.where(low_head, q, zero), jnp.where(low_head, zero, q)], axis=0)
        zs_ref[p] = _dot(qq, kt_ref[0, j, hp])
        return advance(m1), m1, e1, metas[2], metas[3]

    def steps(count, metas):
        for s in range(count):
            metas = substep(s % 2, metas)
        return metas

    n_steps = nhp * (nq * (nq + 1) // 2) + 4
    assert n_steps % 2 == 0
    m0 = (jnp.int32(0), jnp.int32(0), jnp.int32(0))
    metas = lax.fori_loop(0, n_steps // ATTN_UNROLL, lambda _, m: steps(ATTN_UNROLL, m), (m0,) * 5)
    steps(n_steps % ATTN_UNROLL, metas)


def _attn(q, kt, v, u, cap):
    bsz, nhp, seq, _ = q.shape
    t = ATTN_TILE
    nq = seq // t
    whole = lambda shape: pl.BlockSpec((1,) + shape, lambda b: (b,) + (0,) * len(shape))
    pair = lambda dt: pltpu.VMEM((2, 2 * t, t), dt)
    return pl.pallas_call(
        functools.partial(_attn_body, nq=nq, nhp=nhp),
        grid=(bsz,),
        in_specs=[
            whole((nhp, seq, LANES)),
            whole((nq, nhp, LANES, t)),
            whole((nhp, seq, LANES)),
            pl.BlockSpec((t, t), lambda b: (0, 0)),
            pl.BlockSpec((2, 2 * t, t), lambda b: (0, 0, 0)),
        ],
        out_specs=whole((nhp, seq, LANES)),
        out_shape=jax.ShapeDtypeStruct((bsz, nhp, seq, LANES), BF16),
        scratch_shapes=[pair(F32), pair(BF16), pair(F32), pair(F32), pair(BF16),
                        pltpu.VMEM((2 * t, LANES), F32), pltpu.VMEM((2 * t, LANES), F32)],
        compiler_params=_params("parallel"),
        name="attn",
    )(q, kt.reshape(bsz, nq, nhp, LANES, t), v, u, cap)


def _mix_body(x_ref, a_ref, c_ref, ch_ref, cw_ref, cb_ref, lg_ref, lb_ref, wo_ref, o_ref,
              xp_ref, cs_ref, *, aw):
    tm = x_ref.shape[1]
    kw = cw_ref.shape[1]
    si = pl.program_id(1)
    ngrp = xp_ref.shape[0]
    for g in range(ngrp):
        lanes = slice(g * LANES, (g + 1) * LANES)
        xp_ref[g, 0:CONV_HALO] = jnp.where(si > 0, ch_ref[0, :, lanes], 0.0)
        xp_ref[g, CONV_HALO:] = c_ref[0, :, lanes]
    bias = cb_ref[0]
    lng = lg_ref[0]
    lnb = lb_ref[0]
    off = CONV_HALO - (kw - 1)
    for r0 in range(0, tm, CONV_ROWS):
        parts = []
        for g in range(ngrp):
            lanes = slice(g * LANES, (g + 1) * LANES)
            part = jnp.broadcast_to(bias[:, lanes], (CONV_ROWS, LANES))
            for k in range(kw):
                part = part + cw_ref[0, k:k + 1, lanes] * xp_ref[g, r0 + off + k:r0 + off + k + CONV_ROWS, :]
            parts.append(part)
        acc = jnp.concatenate(parts, axis=-1)
        mu = jnp.mean(acc, axis=-1, keepdims=True)
        xc = acc - mu
        var = jnp.mean(xc * xc, axis=-1, keepdims=True)
        y = xc * lax.rsqrt(var + EPS) * lng + lnb
        cs_ref[r0:r0 + CONV_ROWS] = (y * jax.nn.sigmoid(y)).astype(BF16)
    attn = jnp.concatenate([a_ref[0, p] for p in range(a_ref.shape[1])], axis=-1)
    mixed = _dot(attn, wo_ref[0, 0:aw, :]) + _dot(cs_ref[...], wo_ref[0, aw:, :])
    o_ref[0] = x_ref[0] + mixed


def _mix(x, attn, c, conv_w, conv_b, ln_g, ln_b, w_out, layer):
    bsz, seq, d = x.shape
    nhp = attn.shape[1]
    aw = nhp * LANES
    cw = c.shape[2]
    kw = conv_w.shape[1]
    tm = min(MIX_ROW_TILE, seq)
    hb = tm // CONV_HALO
    vec = lambda n: pl.BlockSpec((1, 1, n), lambda b, s: (layer, 0, 0))
    return pl.pallas_call(
        functools.partial(_mix_body, aw=aw),
        grid=(bsz, seq // tm),
        in_specs=[
            pl.BlockSpec((1, tm, d), lambda b, s: (b, s, 0)),
            pl.BlockSpec((1, nhp, tm, LANES), lambda b, s: (b, 0, s, 0)),
            pl.BlockSpec((1, tm, cw), lambda b, s: (b, s, 0)),
            pl.BlockSpec((1, CONV_HALO, cw), lambda b, s: (b, jnp.maximum(s * hb - 1, 0), 0)),
            pl.BlockSpec((1, kw, cw), lambda b, s: (layer, 0, 0)),
            vec(cw), vec(cw), vec(cw),
            pl.BlockSpec((1, d, d), lambda b, s: (layer, 0, 0), pipeline_mode=pl.Buffered(1)),
        ],
        out_specs=pl.BlockSpec((1, tm, d), lambda b, s: (b, s, 0)),
        out_shape=jax.ShapeDtypeStruct((bsz, seq, d), F32),
        scratch_shapes=[pltpu.VMEM((cw // LANES, tm + CONV_HALO, LANES), F32),
                        pltpu.VMEM((tm, cw), BF16)],
        compiler_params=_params("parallel", "parallel"),
        name="mix",
    )(x, attn, c, c, conv_w, conv_b, ln_g, ln_b, w_out)


def _ffn_body(x_ref, xh_ref, g2_ref, wu_ref, fw_ref, fb_ref, wd_ref, o_ref, xe_ref, h_ref, acc_ref,
              graw_ref, vraw_ref, act_ref, *, dff):
    tm = x_ref.shape[1]
    n = FFN_CHUNK
    nchunks = dff // n
    si = pl.program_id(1)
    xe_ref[0:FFN_HALO] = jnp.where(si > 0, xh_ref[0], 0.0)
    xe_ref[FFN_HALO:] = x_ref[0]
    xe = xe_ref[...]
    ms = jnp.mean(xe * xe, axis=-1, keepdims=True)
    h_ref[...] = (xe * lax.rsqrt(ms + EPS) * g2_ref[0]).astype(BF16)
    acc_ref[...] = jnp.zeros_like(acc_ref)

    def col(c, base=0):
        return base + c * n

    def step(c, p, up, gate, down):
        if gate:
            fw = fw_ref[0, :, pl.ds(col(c - 1), n)]
            fb = fb_ref[0, :, pl.ds(col(c - 1), n)]
            taps = fw.shape[0]
            for lg in range(n // LANES):
                lanes = slice(lg * LANES, (lg + 1) * LANES)
                conv = fb[:, lanes]
                for k in range(taps):
                    r0 = FFN_HALO - (taps - 1 - k)
                    conv = conv + fw[k:k + 1, lanes] * graw_ref[1 - p, lg, r0:r0 + tm, :]
                act_ref[1 - p, :, lanes] = (
                    conv * jax.nn.sigmoid(conv) * vraw_ref[1 - p, :, lanes]).astype(BF16)
        if down:
            acc_ref[...] += _dot(act_ref[p], wd_ref[0, pl.ds(col(c - 2), n), :])
        if up:
            g = _dot(h_ref[...], wu_ref[0, :, pl.ds(col(c), n)])
            for lg in range(n // LANES):
                graw_ref[p, lg] = g[:, lg * LANES:(lg + 1) * LANES]
            vraw_ref[p] = _dot(h_ref[FFN_HALO:], wu_ref[0, :, pl.ds(col(c, dff), n)])

    for c in range(nchunks + 2):
        step(c, c % 2, up=c < nchunks, gate=1 <= c <= nchunks, down=c >= 2)
    o_ref[0] = x_ref[0] + acc_ref[...]


def _ffn(x, g2, w_up, ffn_w, ffn_b, w_down, layer):
    bsz, seq, d = x.shape
    dff = w_down.shape[1]
    tm = min(ROW_TILE, seq)
    hb = tm // FFN_HALO
    return pl.pallas_call(
        functools.partial(_ffn_body, dff=dff),
        grid=(bsz, seq // tm),
        in_specs=[
            pl.BlockSpec((1, tm, d), lambda b, s: (b, s, 0)),
            pl.BlockSpec((1, FFN_HALO, d), lambda b, s: (b, jnp.maximum(s * hb - 1, 0), 0)),
            pl.BlockSpec((1, 1, d), lambda b, s: (layer, 0, 0)),
            pl.BlockSpec((1, d, 2 * dff), lambda b, s: (layer, 0, 0), pipeline_mode=pl.Buffered(1)),
            pl.BlockSpec((1, ffn_w.shape[1], dff), lambda b, s: (layer, 0, 0)),
            pl.BlockSpec((1, 1, dff), lambda b, s: (layer, 0, 0)),
            pl.BlockSpec((1, dff, d), lambda b, s: (layer, 0, 0), pipeline_mode=pl.Buffered(1)),
        ],
        out_specs=pl.BlockSpec((1, tm, d), lambda b, s: (b, s, 0)),
        out_shape=jax.ShapeDtypeStruct((bsz, seq, d), F32),
        scratch_shapes=[
            pltpu.VMEM((tm + FFN_HALO, d), F32),
            pltpu.VMEM((tm + FFN_HALO, d), BF16),
            pltpu.VMEM((tm, d), F32),
            pltpu.VMEM((2, FFN_CHUNK // LANES, tm + FFN_HALO, LANES), F32),
            pltpu.VMEM((2, tm, FFN_CHUNK), F32),
            pltpu.VMEM((2, tm, FFN_CHUNK), BF16),
        ],
        compiler_params=_params("parallel", "parallel"),
        name="ffn",
    )(x, x, g2, w_up, ffn_w, ffn_b, w_down)


def kernel(x, norm1_g, w_in, q_norm_g, k_norm_g, conv_dw_w, conv_dw_b, conv_ln_g, conv_ln_b,
           w_out, norm2_g, w_up, ffn_dw_w, ffn_dw_b, w_down):
    depth = w_in.shape[0]
    aw = w_out.shape[1] - conv_dw_w.shape[2]
    heads = aw // HEAD_DIM
    row3 = lambda p: p[:, None, :]

    w_in_b, w_out_b, w_up_b, w_down_b = (w.astype(BF16) for w in (w_in, w_out, w_up, w_down))
    qg = row3(jnp.tile(q_norm_g, (1, heads)))
    kg = row3(jnp.tile(k_norm_g, (1, heads)))
    g1, g2 = row3(norm1_g), row3(norm2_g)
    cb, lg, lb, fb = row3(conv_dw_b), row3(conv_ln_g), row3(conv_ln_b), row3(ffn_dw_b)

    idx = jnp.arange(2 * LANES)
    bd = (idx[:, None] // HEAD_DIM == idx[None, :] // HEAD_DIM).astype(BF16)
    tix = jnp.arange(ATTN_TILE)
    u = (tix[:, None] > tix[None, :]).astype(BF16)
    causal_cap = jnp.where(tix[None, :] < tix[:, None], F32_MAX, -F32_MAX).astype(F32)
    causal_cap = jnp.concatenate([causal_cap, causal_cap], axis=0)
    cap = jnp.stack([jnp.full_like(causal_cap, F32_MAX), causal_cap])

    for layer in range(depth):
        q, kt, v, c = _proj(x, g1, w_in_b, qg, kg, bd, layer, aw)
        attn = _attn(q, kt, v, u, cap)
        x = _mix(x, attn, c, conv_dw_w, cb, lg, lb, w_out_b, layer)
        x = _ffn(x, g2, w_up_b, ffn_dw_w, fb, w_down_b, layer)
    return x
```

```python
import functools

import jax
import jax.numpy as jnp
from jax import lax
from jax.experimental import pallas as pl
from jax.experimental.pallas import tpu as pltpu

F32 = jnp.float32
BF16 = jnp.bfloat16

EPS = 1e-6
HEAD_DIM = 64
LANES = 128
ATTN_TILE = 256
ATTN_UNROLL = 36
ROW_TILE = 1024
MIX_ROW_TILE = 512
CONV_HALO = 32
FFN_HALO = 16
FFN_CHUNK = 256
CONV_ROWS = 32
VMEM_LIMIT = 56 * 1024 * 1024
F32_MAX = 3.4028234663852886e38
LOG2E = 1.4426950408889634


def _params(*sem):
    return pltpu.CompilerParams(dimension_semantics=sem, vmem_limit_bytes=VMEM_LIMIT)


def _dot(a, b):
    return jnp.dot(a, b, preferred_element_type=F32)


def _proj_body(x_ref, g1_ref, w_ref, qg_ref, kg_ref, bd_ref, q_ref, kt_ref, v_ref, c_ref, *, aw):
    x = x_ref[0]
    ms = jnp.mean(x * x, axis=-1, keepdims=True)
    h = (x * lax.rsqrt(ms + EPS) * g1_ref[0]).astype(BF16)

    def head_norm(t, g):
        sq = (t * t).astype(BF16)
        half = 2 * LANES
        ss = jnp.concatenate(
            [_dot(sq[:, i * half:(i + 1) * half], bd_ref[...]) for i in range(aw // half)], axis=-1)
        return t * lax.rsqrt(ss * (1.0 / HEAD_DIM) + EPS) * g

    cw = (w_ref.shape[2] - 3 * aw) // 2
    a = _dot(h, w_ref[0, :, 3 * aw:3 * aw + cw])
    b = _dot(h, w_ref[0, :, 3 * aw + cw:3 * aw + 2 * cw])
    c_ref[0] = a * jax.nn.sigmoid(b)
    q = head_norm(_dot(h, w_ref[0, :, 0:aw]), qg_ref[0])
    q = (q * (HEAD_DIM ** -0.5 * LOG2E)).astype(BF16)
    k = head_norm(_dot(h, w_ref[0, :, aw:2 * aw]), kg_ref[0])
    for i in range(k.shape[0] // ATTN_TILE):
        kt_ref[0, i] = k[i * ATTN_TILE:(i + 1) * ATTN_TILE, :].T.astype(BF16)
    v = _dot(h, w_ref[0, :, 2 * aw:3 * aw]).astype(BF16)
    for p in range(aw // LANES):
        q_ref[0, p] = q[:, p * LANES:(p + 1) * LANES]
        v_ref[0, p] = v[:, p * LANES:(p + 1) * LANES]


def _proj(x, g1, w_in, qg, kg, bd, layer, aw):
    bsz, seq, d = x.shape
    tm = min(ROW_TILE, seq)
    cw = (w_in.shape[2] - 3 * aw) // 2
    nkt = tm // ATTN_TILE
    nhp = aw // LANES
    return pl.pallas_call(
        functools.partial(_proj_body, aw=aw),
        grid=(bsz, seq // tm),
        in_specs=[
            pl.BlockSpec((1, tm, d), lambda b, s: (b, s, 0)),
            pl.BlockSpec((1, 1, d), lambda b, s: (layer, 0, 0)),
            pl.BlockSpec((1, d, w_in.shape[2]), lambda b, s: (layer, 0, 0),
                         pipeline_mode=pl.Buffered(1)),
            pl.BlockSpec((1, 1, aw), lambda b, s: (layer, 0, 0)),
            pl.BlockSpec((1, 1, aw), lambda b, s: (layer, 0, 0)),
            pl.BlockSpec(bd.shape, lambda b, s: (0, 0)),
        ],
        out_specs=[
            pl.BlockSpec((1, nhp, tm, LANES), lambda b, s: (b, 0, s, 0)),
            pl.BlockSpec((1, nkt, aw, ATTN_TILE), lambda b, s: (b, s, 0, 0)),
            pl.BlockSpec((1, nhp, tm, LANES), lambda b, s: (b, 0, s, 0)),
            pl.BlockSpec((1, tm, cw), lambda b, s: (b, s, 0)),
        ],
        out_shape=[
            jax.ShapeDtypeStruct((bsz, nhp, seq, LANES), BF16),
            jax.ShapeDtypeStruct((bsz, seq // ATTN_TILE, aw, ATTN_TILE), BF16),
            jax.ShapeDtypeStruct((bsz, nhp, seq, LANES), BF16),
            jax.ShapeDtypeStruct((bsz, seq, cw), F32),
        ],
        compiler_params=_params("parallel", "parallel"),
        name="proj",
    )(x, g1, w_in, qg, kg, bd)


def _attn_body(q_ref, kt_ref, v_ref, u_ref, cap_ref, o_ref, zs_ref, hi_ref, xb_ref, ts_ref, w_ref,
               acc_ref, carry_ref, ec_ref, *, nq, nhp):
    t = ATTN_TILE
    lane = lax.broadcasted_iota(jnp.int32, (t, LANES), 1)
    low_head = lane < HEAD_DIM
    for ref in (zs_ref, hi_ref, ts_ref, w_ref, acc_ref, carry_ref, ec_ref):
        ref[...] = jnp.zeros_like(ref)
    xb_ref[...] = jnp.full(xb_ref.shape, -F32_MAX, F32)

    def advance(meta):
        hp, qi, j = meta
        row_done = j == 0
        qn = jnp.where(row_done, qi + 1, qi)
        jn = jnp.where(row_done, qi + 1, j - 1)
        pair_done = qn >= nq
        hn = jnp.where(pair_done, hp + 1, hp)
        qn = jnp.where(pair_done, 0, qn)
        jn = jnp.where(pair_done, 0, jn)
        over = hn >= nhp
        return jnp.where(over, nhp - 1, hn), jnp.where(over, nq - 1, qn), jnp.where(over, 0, jn)

    def substep(s, metas):
        p = s % 2
        m1, e1, _, _, m3 = metas
        hp, qi, j = m3
        pv = _dot(w_ref[p], v_ref[0, hp, pl.ds(pl.multiple_of(j * t, t), t), :])
        pv = pv * ec_ref[(s + 1) % 4]
        acc = jnp.where(j == qi, pv, acc_ref[...] + pv)
        acc_ref[...] = acc
        o_ref[0, hp, pl.ds(pl.multiple_of(qi * t, t), t), :] = (
            jnp.where(low_head, acc[0:t], acc[t:2 * t]).astype(o_ref.dtype))
        w_ref[1 - p] = jnp.exp2(xb_ref[1 - p] + ts_ref[1 - p]).astype(BF16)
        ts_ref[p] = _dot(hi_ref[p], u_ref[...])
        diag = e1[2] == e1[1]
        z = jnp.minimum(zs_ref[1 - p], cap_ref[diag.astype(jnp.int32)])
        neg_abs = lax.bitcast_convert_type(
            lax.bitcast_convert_type(z, jnp.uint32) | jnp.uint32(0x80000000), F32)
        lb = jnp.minimum(z, 0.0) - jnp.log2(1.0 + jnp.exp2(neg_abs))
        l1 = lb - z
        hi_ref[1 - p] = l1.astype(BF16)
        cin = jnp.where(diag, 0.0, carry_ref[...])
        xb_ref[1 - p] = lb
        ec_ref[s % 4] = jnp.exp2(cin)
        carry_ref[...] = cin + jnp.sum(l1, axis=-1, keepdims=True)
        hp, qi, j = m1
        q = q_ref[0, hp, pl.ds(pl.multiple_of(qi * t, t), t), :]
        zero = jnp.zeros_like(q)
        qq = jnp.concatenate([jnp.where(low_head, q, zero), jnp.where(low_head, zero, q)], axis=0)
        zs_ref[p] = _dot(qq, kt_ref[0, j, hp])
        return advance(m1), m1, e1, metas[2], metas[3]

    def steps(count, metas):
        for s in range(count):
            metas = substep(s, metas)
        return metas

    n_steps = nhp * (nq * (nq + 1) // 2) + 4
    assert n_steps % 4 == 0 and ATTN_UNROLL % 4 == 0
    m0 = (jnp.int32(0), jnp.int32(0), jnp.int32(0))
    metas = lax.fori_loop(0, n_steps // ATTN_UNROLL, lambda _, m: steps(ATTN_UNROLL, m), (m0,) * 5)
    steps(n_steps % ATTN_UNROLL, metas)


def _attn(q, kt, v, u, cap):
    bsz, nhp, seq, _ = q.shape
    t = ATTN_TILE
    nq = seq // t
    whole = lambda shape: pl.BlockSpec((1,) + shape, lambda b: (b,) + (0,) * len(shape))
    pair = lambda dt: pltpu.VMEM((2, 2 * t, t), dt)
    return pl.pallas_call(
        functools.partial(_attn_body, nq=nq, nhp=nhp),
        grid=(bsz,),
        in_specs=[
            whole((nhp, seq, LANES)),
            whole((nq, nhp, LANES, t)),
            whole((nhp, seq, LANES)),
            pl.BlockSpec((t, t), lambda b: (0, 0)),
            pl.BlockSpec((2, 2 * t, t), lambda b: (0, 0, 0)),
        ],
        out_specs=whole((nhp, seq, LANES)),
        out_shape=jax.ShapeDtypeStruct((bsz, nhp, seq, LANES), BF16),
        scratch_shapes=[pair(F32), pair(BF16), pair(F32), pair(F32), pair(BF16),
                        pltpu.VMEM((2 * t, LANES), F32), pltpu.VMEM((2 * t, LANES), F32),
                        pltpu.VMEM((4, 2 * t, LANES), F32)],
        compiler_params=_params("parallel"),
        name="attn",
    )(q, kt.reshape(bsz, nq, nhp, LANES, t), v, u, cap)


def _mix_body(x_ref, a_ref, c_ref, ch_ref, cw_ref, cb_ref, lg_ref, lb_ref, wo_ref, o_ref,
              xp_ref, cs_ref, *, aw):
    tm = x_ref.shape[1]
    kw = cw_ref.shape[1]
    si = pl.program_id(1)
    ngrp = xp_ref.shape[0]
    for g in range(ngrp):
        lanes = slice(g * LANES, (g + 1) * LANES)
        xp_ref[g, 0:CONV_HALO] = jnp.where(si > 0, ch_ref[0, :, lanes], 0.0)
        xp_ref[g, CONV_HALO:] = c_ref[0, :, lanes]
    bias = cb_ref[0]
    lng = lg_ref[0]
    lnb = lb_ref[0]
    off = CONV_HALO - (kw - 1)
    for r0 in range(0, tm, CONV_ROWS):
        parts = []
        for g in range(ngrp):
            lanes = slice(g * LANES, (g + 1) * LANES)
            part = jnp.broadcast_to(bias[:, lanes], (CONV_ROWS, LANES))
            for k in range(kw):
                part = part + cw_ref[0, k:k + 1, lanes] * xp_ref[g, r0 + off + k:r0 + off + k + CONV_ROWS, :]
            parts.append(part)
        acc = jnp.concatenate(parts, axis=-1)
        mu = jnp.mean(acc, axis=-1, keepdims=True)
        xc = acc - mu
        var = jnp.mean(xc * xc, axis=-1, keepdims=True)
        y = xc * lax.rsqrt(var + EPS) * lng + lnb
        cs_ref[r0:r0 + CONV_ROWS] = (y * jax.nn.sigmoid(y)).astype(BF16)
    attn = jnp.concatenate([a_ref[0, p] for p in range(a_ref.shape[1])], axis=-1)
    mixed = _dot(attn, wo_ref[0, 0:aw, :]) + _dot(cs_ref[...], wo_ref[0, aw:, :])
    o_ref[0] = x_ref[0] + mixed


def _mix(x, attn, c, conv_w, conv_b, ln_g, ln_b, w_out, layer):
    bsz, seq, d = x.shape
    nhp = attn.shape[1]
    aw = nhp * LANES
    cw = c.shape[2]
    kw = conv_w.shape[1]
    tm = min(MIX_ROW_TILE, seq)
    hb = tm // CONV_HALO
    vec = lambda n: pl.BlockSpec((1, 1, n), lambda b, s: (layer, 0, 0))
    return pl.pallas_call(
        functools.partial(_mix_body, aw=aw),
        grid=(bsz, seq // tm),
        in_specs=[
            pl.BlockSpec((1, tm, d), lambda b, s: (b, s, 0)),
            pl.BlockSpec((1, nhp, tm, LANES), lambda b, s: (b, 0, s, 0)),
            pl.BlockSpec((1, tm, cw), lambda b, s: (b, s, 0)),
            pl.BlockSpec((1, CONV_HALO, cw), lambda b, s: (b, jnp.maximum(s * hb - 1, 0), 0)),
            pl.BlockSpec((1, kw, cw), lambda b, s: (layer, 0, 0)),
            vec(cw), vec(cw), vec(cw),
            pl.BlockSpec((1, d, d), lambda b, s: (layer, 0, 0), pipeline_mode=pl.Buffered(1)),
        ],
        out_specs=pl.BlockSpec((1, tm, d), lambda b, s: (b, s, 0)),
        out_shape=jax.ShapeDtypeStruct((bsz, seq, d), F32),
        scratch_shapes=[pltpu.VMEM((cw // LANES, tm + CONV_HALO, LANES), F32),
                        pltpu.VMEM((tm, cw), BF16)],
        compiler_params=_params("parallel", "parallel"),
        name="mix",
    )(x, attn, c, c, conv_w, conv_b, ln_g, ln_b, w_out)


def _ffn_body(x_ref, xh_ref, g2_ref, wu_ref, fw_ref, fb_ref, wd_ref, o_ref, xe_ref, h_ref, acc_ref,
              graw_ref, vraw_ref, act_ref, *, dff):
    tm = x_ref.shape[1]
    n = FFN_CHUNK
    nchunks = dff // n
    si = pl.program_id(1)
    xe_ref[0:FFN_HALO] = jnp.where(si > 0, xh_ref[0], 0.0)
    xe_ref[FFN_HALO:] = x_ref[0]
    xe = xe_ref[...]
    ms = jnp.mean(xe * xe, axis=-1, keepdims=True)
    h_ref[...] = (xe * lax.rsqrt(ms + EPS) * g2_ref[0]).astype(BF16)
    acc_ref[...] = jnp.zeros_like(acc_ref)

    def col(c, base=0):
        return base + c * n

    def step(c, p, up, gate, down):
        if gate:
            fw = fw_ref[0, :, pl.ds(col(c - 1), n)]
            fb = fb_ref[0, :, pl.ds(col(c - 1), n)]
            taps = fw.shape[0]
            for lg in range(n // LANES):
                lanes = slice(lg * LANES, (lg + 1) * LANES)
                conv = fb[:, lanes]
                for k in range(taps):
                    r0 = FFN_HALO - (taps - 1 - k)
                    conv = conv + fw[k:k + 1, lanes] * graw_ref[1 - p, lg, r0:r0 + tm, :]
                act_ref[1 - p, :, lanes] = (
                    conv * jax.nn.sigmoid(conv) * vraw_ref[1 - p, :, lanes]).astype(BF16)
        if down:
            acc_ref[...] += _dot(act_ref[p], wd_ref[0, pl.ds(col(c - 2), n), :])
        if up:
            g = _dot(h_ref[...], wu_ref[0, :, pl.ds(col(c), n)])
            for lg in range(n // LANES):
                graw_ref[p, lg] = g[:, lg * LANES:(lg + 1) * LANES]
            vraw_ref[p] = _dot(h_ref[FFN_HALO:], wu_ref[0, :, pl.ds(col(c, dff), n)])

    for c in range(nchunks + 2):
        step(c, c % 2, up=c < nchunks, gate=1 <= c <= nchunks, down=c >= 2)
    o_ref[0] = x_ref[0] + acc_ref[...]


def _ffn(x, g2, w_up, ffn_w, ffn_b, w_down, layer):
    bsz, seq, d = x.shape
    dff = w_down.shape[1]
    tm = min(ROW_TILE, seq)
    hb = tm // FFN_HALO
    return pl.pallas_call(
        functools.partial(_ffn_body, dff=dff),
        grid=(bsz, seq // tm),
        in_specs=[
            pl.BlockSpec((1, tm, d), lambda b, s: (b, s, 0)),
            pl.BlockSpec((1, FFN_HALO, d), lambda b, s: (b, jnp.maximum(s * hb - 1, 0), 0)),
            pl.BlockSpec((1, 1, d), lambda b, s: (layer, 0, 0)),
            pl.BlockSpec((1, d, 2 * dff), lambda b, s: (layer, 0, 0), pipeline_mode=pl.Buffered(1)),
            pl.BlockSpec((1, ffn_w.shape[1], dff), lambda b, s: (layer, 0, 0)),
            pl.BlockSpec((1, 1, dff), lambda b, s: (layer, 0, 0)),
            pl.BlockSpec((1, dff, d), lambda b, s: (layer, 0, 0), pipeline_mode=pl.Buffered(1)),
        ],
        out_specs=pl.BlockSpec((1, tm, d), lambda b, s: (b, s, 0)),
        out_shape=jax.ShapeDtypeStruct((bsz, seq, d), F32),
        scratch_shapes=[
            pltpu.VMEM((tm + FFN_HALO, d), F32),
            pltpu.VMEM((tm + FFN_HALO, d), BF16),
            pltpu.VMEM((tm, d), F32),
            pltpu.VMEM((2, FFN_CHUNK // LANES, tm + FFN_HALO, LANES), F32),
            pltpu.VMEM((2, tm, FFN_CHUNK), F32),
            pltpu.VMEM((2, tm, FFN_CHUNK), BF16),
        ],
        compiler_params=_params("parallel", "parallel"),
        name="ffn",
    )(x, x, g2, w_up, ffn_w, ffn_b, w_down)


def kernel(x, norm1_g, w_in, q_norm_g, k_norm_g, conv_dw_w, conv_dw_b, conv_ln_g, conv_ln_b,
           w_out, norm2_g, w_up, ffn_dw_w, ffn_dw_b, w_down):
    depth = w_in.shape[0]
    aw = w_out.shape[1] - conv_dw_w.shape[2]
    heads = aw // HEAD_DIM
    row3 = lambda p: p[:, None, :]

    w_in_b, w_out_b, w_up_b, w_down_b = (w.astype(BF16) for w in (w_in, w_out, w_up, w_down))
    qg = row3(jnp.tile(q_norm_g, (1, heads)))
    kg = row3(jnp.tile(k_norm_g, (1, heads)))
    g1, g2 = row3(norm1_g), row3(norm2_g)
    cb, lg, lb, fb = row3(conv_dw_b), row3(conv_ln_g), row3(conv_ln_b), row3(ffn_dw_b)

    idx = jnp.arange(2 * LANES)
    bd = (idx[:, None] // HEAD_DIM == idx[None, :] // HEAD_DIM).astype(BF16)
    tix = jnp.arange(ATTN_TILE)
    u = (tix[:, None] > tix[None, :]).astype(BF16)
    causal_cap = jnp.where(tix[None, :] < tix[:, None], F32_MAX, -F32_MAX).astype(F32)
    causal_cap = jnp.concatenate([causal_cap, causal_cap], axis=0)
    cap = jnp.stack([jnp.full_like(causal_cap, F32_MAX), causal_cap])

    for layer in range(depth):
        q, kt, v, c = _proj(x, g1, w_in_b, qg, kg, bd, layer, aw)
        attn = _attn(q, kt, v, u, cap)
        x = _mix(x, attn, c, conv_dw_w, cb, lg, lb, w_out_b, layer)
        x = _ffn(x, g2, w_up_b, ffn_dw_w, fb, w_down_b, layer)
    return x
```
